```python
import math
import jax, jax.numpy as jnp
from jax import lax
import numpy as np

D_MODEL = 1024
BATCH = 8
SEQ = 4096
DEPTH = 2

N_MIXERS = 2
N_RWKV_LAYERS = (DEPTH + 1) // 2
N_DSA_LAYERS = DEPTH // 2
N_META = 16
D_FF = 2816
NORM_EPS = 1e-6
HEAD_DIM = 64
ROPE_DIM = HEAD_DIM // 4
ROPE_THETA = 500000.0
RWKV_HEADS = D_MODEL // HEAD_DIM
D_DECAY_LORA = 64
D_AAA_LORA = 64
D_GATE_LORA = 160
RWKV_LN_EPS = 64e-5
ATT_HEADS = D_MODEL // HEAD_DIM
ATT_KV_HEADS = 4
ATT_GROUP = ATT_HEADS // ATT_KV_HEADS
IDX_HEADS = 8
IDX_DIM = 64
TOPK_MAX = 256
Q_BLOCK = 128
ATT_Q_W = ATT_HEADS * HEAD_DIM
ATT_KV_W = ATT_KV_HEADS * HEAD_DIM
IDX_Q_W = IDX_HEADS * IDX_DIM
ATT_SPLITS = (ATT_Q_W, ATT_Q_W + ATT_KV_W, ATT_Q_W + 2 * ATT_KV_W,
              ATT_Q_W + 2 * ATT_KV_W + IDX_Q_W, ATT_Q_W + 2 * ATT_KV_W + IDX_Q_W + IDX_DIM)
ATT_IN_W = ATT_Q_W + 2 * ATT_KV_W + IDX_Q_W + IDX_DIM + IDX_HEADS

kernel_name = "hybrid_rwkv7_dsa_macaron"


def rms_norm(x, g, eps=NORM_EPS):
    xf = x.astype(jnp.float32)
    y = xf * lax.rsqrt(jnp.mean(xf * xf, axis=-1, keepdims=True) + eps)
    return y.astype(x.dtype) * g


def swiglu(h, w_in, w_out):
    gate, up = jnp.split(h @ w_in, 2, axis=-1)
    return (jax.nn.silu(gate) * up) @ w_out


def rope_tables(n_pos, dtype):
    inv = ROPE_THETA ** (-jnp.arange(0, ROPE_DIM, 2, dtype=jnp.float32) / ROPE_DIM)
    ang = jnp.arange(n_pos, dtype=jnp.float32)[:, None] * inv[None, :]
    return jnp.cos(ang).astype(dtype), jnp.sin(ang).astype(dtype)


def partial_rope(x, cos, sin):
    half = ROPE_DIM // 2
    x1 = x[..., :half]
    x2 = x[..., half:ROPE_DIM]
    c = cos[None, :, None, :]
    s = sin[None, :, None, :]
    return jnp.concatenate([x1 * c - x2 * s, x2 * c + x1 * s, x[..., ROPE_DIM:]], axis=-1)


def wkv7_scan(r, w, k, v, a, b):
    B, T, H, N = r.shape
    xs = tuple(jnp.moveaxis(t.astype(jnp.float32), 1, 0) for t in (r, w, k, v, a, b))

    def step(S, inp):
        r_t, w_t, k_t, v_t, a_t, b_t = inp
        sa = jnp.einsum("bhij,bhj->bhi", S, a_t)
        S = S * w_t[:, :, None, :] + sa[..., None] * b_t[:, :, None, :] + v_t[..., None] * k_t[:, :, None, :]
        y = jnp.einsum("bhij,bhj->bhi", S, r_t)
        return S, y

    S0 = jnp.zeros((B, H, N, N), jnp.float32)
    _, ys = lax.scan(step, S0, xs)
    return jnp.moveaxis(ys, 0, 1).astype(r.dtype)


def rwkv7_time_mix(h, mix, w_r, w_k, w_v, w_o, w0, w1, w2, a0, a1, a2, g1, g2,
                   k_k, k_a, r_k, lnx_g, lnx_b):
    B, T, D = h.shape
    H, N = RWKV_HEADS, HEAD_DIM
    xx = jnp.pad(h, ((0, 0), (1, 0), (0, 0)))[:, :-1] - h
    xm = h[None] + xx[None] * mix[:, None, None, :]
    xr, xw, xk, xv, xa, xg = xm[0], xm[1], xm[2], xm[3], xm[4], xm[5]
    r = xr @ w_r
    k = xk @ w_k
    v = xv @ w_v
    w_log = -jax.nn.softplus(-(w0 + jnp.tanh(xw @ w1) @ w2)) - 0.5
    a = jax.nn.sigmoid(a0 + (xa @ a1) @ a2)
    g = jax.nn.sigmoid(xg @ g1) @ g2
    kk = (k * k_k).reshape(B, T, H, N).astype(jnp.float32)
    kk = (kk / jnp.maximum(jnp.sqrt(jnp.sum(kk * kk, axis=-1, keepdims=True)), 1e-12)).astype(h.dtype)
    k = k * (1.0 + (a - 1.0) * k_a)
    decay = jnp.exp(-jnp.exp(w_log.astype(jnp.float32)))
    rh = r.reshape(B, T, H, N)
    kh = k.reshape(B, T, H, N)
    vh = v.reshape(B, T, H, N)
    ah = a.reshape(B, T, H, N)
    y = wkv7_scan(rh, decay.reshape(B, T, H, N), kh, vh, -kk, kk * ah)
    yf = y.astype(jnp.float32)
    mu = jnp.mean(yf, axis=-1, keepdims=True)
    var = jnp.mean((yf - mu) ** 2, axis=-1, keepdims=True)
    yn = ((yf - mu) * lax.rsqrt(var + RWKV_LN_EPS)).astype(h.dtype).reshape(B, T, D) * lnx_g + lnx_b
    bonus = (jnp.sum(rh * kh * r_k, axis=-1, keepdims=True) * vh).reshape(B, T, D)
    return ((yn + bonus) * g) @ w_o


def dsa_attention(h, w_in, q_g, k_g, w_o, cos, sin, topk):
    B, T, D = h.shape
    q, k, v, qi, ki, wi = jnp.split(h @ w_in, ATT_SPLITS, axis=-1)
    q = partial_rope(rms_norm(q.reshape(B, T, ATT_HEADS, HEAD_DIM), q_g), cos, sin)
    k = partial_rope(rms_norm(k.reshape(B, T, ATT_KV_HEADS, HEAD_DIM), k_g), cos, sin)
    v = v.reshape(B, T, ATT_KV_HEADS, HEAD_DIM)
    qi = partial_rope(qi.reshape(B, T, IDX_HEADS, IDX_DIM), cos, sin)
    ki = partial_rope(ki[:, :, None, :], cos, sin)[:, :, 0, :]
    wi = wi * (IDX_HEADS ** -0.5 * IDX_DIM ** -0.5)

    n_blk = -(-T // Q_BLOCK)
    pad = n_blk * Q_BLOCK - T

    def to_blocks(t):
        t = jnp.pad(t, [(0, 0), (0, pad)] + [(0, 0)] * (t.ndim - 2))
        return jnp.moveaxis(t.reshape((B, n_blk, Q_BLOCK) + t.shape[2:]), 1, 0)

    starts = jnp.arange(n_blk, dtype=jnp.int32) * Q_BLOCK
    kpos = jnp.arange(T, dtype=jnp.int32)
    scale = HEAD_DIM ** -0.5

    def block(args):
        qb, qib, wib, start = args
        qpos = start + jnp.arange(Q_BLOCK, dtype=jnp.int32)
        rel = jax.nn.relu(jnp.einsum("bqhd,bkd->bqhk", qib, ki))
        score = jnp.einsum("bqh,bqhk->bqk", wib, rel).astype(jnp.float32)
        causal = kpos[None, :] <= qpos[:, None]
        score = jnp.where(causal[None], score, -jnp.inf)
        _, idx = lax.top_k(score, topk)
        valid = idx <= qpos[None, :, None]
        kg = jax.vmap(lambda kb, ib: kb[ib])(k, idx)
        vg = jax.vmap(lambda vb, ib: vb[ib])(v, idx)
        qg = qb.reshape(B, Q_BLOCK, ATT_KV_HEADS, ATT_GROUP, HEAD_DIM)
        logits = jnp.einsum("bqgrd,bqkgd->bqgrk", qg, kg).astype(jnp.float32) * scale
        logits = jnp.where(valid[:, :, None, None, :], logits, -jnp.inf)
        p = jax.nn.softmax(logits, axis=-1).astype(v.dtype)
        o = jnp.einsum("bqgrk,bqkgd->bqgrd", p, vg)
        return o.reshape(B, Q_BLOCK, ATT_Q_W)

    out = lax.map(block, (to_blocks(q), to_blocks(qi), to_blocks(wi), starts))
    out = jnp.moveaxis(out, 0, 1).reshape(B, n_blk * Q_BLOCK, ATT_Q_W)[:, :T]
    return out @ w_o


def setup_inputs(seed: int = 0) -> dict:
    key = jax.random.key(seed)
    ks = jax.random.split(key, 32)
    f32 = jnp.float32

    def nrm(k, shape, scale):
        return jax.random.normal(k, shape, f32) * scale

    D = D_MODEL
    A = N_RWKV_LAYERS
    S = N_DSA_LAYERS
    return {
        "x": nrm(ks[0], (BATCH, SEQ, D), 1.0),
        "meta": nrm(ks[1], (N_META, D), 1.0),
        "norm_g": 1.0 + nrm(ks[2], (DEPTH, 3, D), 0.02),
        "ffn_w_in": nrm(ks[3], (DEPTH, 2, D, 2 * D_FF), D ** -0.5),
        "ffn_w_out": nrm(ks[4], (DEPTH, 2, D_FF, D), D_FF ** -0.5),
        "rk_mix": jax.random.uniform(ks[5], (A, 6, D), f32),
        "rk_w_r": nrm(ks[6], (A, D, D), D ** -0.5),
        "rk_w_k": nrm(ks[7], (A, D, D), D ** -0.5),
        "rk_w_v": nrm(ks[8], (A, D, D), D ** -0.5),
        "rk_w_o": nrm(ks[9], (A, D, D), D ** -0.5),
        "rk_w0": -0.6 + nrm(ks[10], (A, D), 0.3),
        "rk_w1": nrm(ks[11], (A, D, D_DECAY_LORA), D ** -0.5),
        "rk_w2": nrm(ks[12], (A, D_DECAY_LORA, D), 0.1 * D_DECAY_LORA ** -0.5),
        "rk_a0": nrm(ks[13], (A, D), 0.1),
        "rk_a1": nrm(ks[14], (A, D, D_AAA_LORA), D ** -0.5),
        "rk_a2": nrm(ks[15], (A, D_AAA_LORA, D), 0.1 * D_AAA_LORA ** -0.5),
        "rk_g1": nrm(ks[16], (A, D, D_GATE_LORA), D ** -0.5),
        "rk_g2": nrm(ks[17], (A, D_GATE_LORA, D), D_GATE_LORA ** -0.5),
        "rk_k_k": 0.85 + nrm(ks[18], (A, D), 0.05),
        "rk_k_a": 1.0 + nrm(ks[19], (A, D), 0.05),
        "rk_r_k": nrm(ks[20], (A, RWKV_HEADS, HEAD_DIM), 0.1),
        "rk_lnx_g": 1.0 + nrm(ks[21], (A, D), 0.02),
        "rk_lnx_b": nrm(ks[22], (A, D), 0.02),
        "at_w_in": nrm(ks[23], (S, D, ATT_IN_W), D ** -0.5),
        "at_q_g": 1.0 + nrm(ks[24], (S, HEAD_DIM), 0.02),
        "at_k_g": 1.0 + nrm(ks[25], (S, HEAD_DIM), 0.02),
        "at_w_o": nrm(ks[26], (S, ATT_Q_W, D), ATT_Q_W ** -0.5),
    }


def reference(x, meta, norm_g, ffn_w_in, ffn_w_out, rk_mix, rk_w_r, rk_w_k, rk_w_v, rk_w_o,
              rk_w0, rk_w1, rk_w2, rk_a0, rk_a1, rk_a2, rk_g1, rk_g2, rk_k_k, rk_k_a, rk_r_k,
              rk_lnx_g, rk_lnx_b, at_w_in, at_q_g, at_k_g, at_w_o):
    B, L, D = x.shape
    topk = min(TOPK_MAX, L // 4)
    h = jnp.concatenate([jnp.broadcast_to(meta.astype(x.dtype)[None], (B, N_META, D)), x], axis=1)
    T = h.shape[1]
    cos, sin = rope_tables(T, x.dtype)
    for i in range(DEPTH):
        g = norm_g[i]
        h = h + 0.5 * swiglu(rms_norm(h, g[0]), ffn_w_in[i, 0], ffn_w_out[i, 0])
        hn = rms_norm(h, g[1])
        j = i // N_MIXERS
        if i % N_MIXERS == 0:
            h = h + rwkv7_time_mix(hn, rk_mix[j], rk_w_r[j], rk_w_k[j], rk_w_v[j], rk_w_o[j],
                                   rk_w0[j], rk_w1[j], rk_w2[j], rk_a0[j], rk_a1[j], rk_a2[j],
                                   rk_g1[j], rk_g2[j], rk_k_k[j], rk_k_a[j], rk_r_k[j],
                                   rk_lnx_g[j], rk_lnx_b[j])
        else:
            h = h + dsa_attention(hn, at_w_in[j], at_q_g[j], at_k_g[j], at_w_o[j], cos, sin, topk)
        h = h + 0.5 * swiglu(rms_norm(h, g[2]), ffn_w_in[i, 1], ffn_w_out[i, 1])
    return h[:, N_META:]
```

```python
import functools

import jax
import jax.numpy as jnp
from jax import lax
from jax.experimental import pallas as pl
from jax.experimental.pallas import tpu as pltpu

D_MODEL = 1024
N_META = 16
D_FF = 2816
NORM_EPS = 1e-6
HEAD_DIM = 64
ROPE_DIM = HEAD_DIM // 4
ROPE_THETA = 500000.0
RWKV_HEADS = D_MODEL // HEAD_DIM
RWKV_LN_EPS = 64e-5
ATT_HEADS = D_MODEL // HEAD_DIM
ATT_KV_HEADS = 4
ATT_GROUP = ATT_HEADS // ATT_KV_HEADS
IDX_HEADS = 8
IDX_DIM = 64
TOPK_MAX = 256
ATT_Q_W = ATT_HEADS * HEAD_DIM
ATT_KV_W = ATT_KV_HEADS * HEAD_DIM
IDX_Q_W = IDX_HEADS * IDX_DIM

LANES = 128
SEQ_ALIGN = 256
VMEM_LIMIT = 56 * 1024 * 1024
FF_CHUNK = 256
WKV_CHUNK = 64
WKV_CHUNKS_PER_STEP = 4
INT_MIN = -(2 ** 31)
NEG_BIG = -1e30

_MXU = jnp.bfloat16
_F32 = jnp.float32


def _dot(a, b):
    return jnp.dot(a.astype(_MXU), b.astype(_MXU), preferred_element_type=_F32)


def _dot_nt(a, b):
    return lax.dot_general(a.astype(_MXU), b.astype(_MXU), (((1,), (1,)), ((), ())),
                           preferred_element_type=_F32)


def _dot_tn(a, b):
    return lax.dot_general(a.astype(_MXU), b.astype(_MXU), (((0,), (0,)), ((), ())),
                           preferred_element_type=_F32)


def _split2(x):
    hi = x.astype(_MXU)
    lo = (x - hi.astype(_F32)).astype(_MXU)
    return hi, lo


def _split3(x):
    hi = x.astype(_MXU)
    r1 = x - hi.astype(_F32)
    mid = r1.astype(_MXU)
    lo = (r1 - mid.astype(_F32)).astype(_MXU)
    return hi, mid, lo


def _rms(x, eps=NORM_EPS):
    return x * lax.rsqrt(jnp.mean(x * x, axis=-1, keepdims=True) + eps)


def _sigmoid(x):
    return 1.0 / (1.0 + jnp.exp(-x))


def _params(*sem):
    return pltpu.CompilerParams(dimension_semantics=sem, vmem_limit_bytes=VMEM_LIMIT)


def _const_spec(shape):
    n = len(shape)
    return pl.BlockSpec(shape, lambda *_: (0,) * n)


def _ffn_body(h_ref, g_ref, win_ref, wout_ref, o_ref, *, n_chunks):
    x = h_ref[...]
    xb = (_rms(x) * g_ref[...]).astype(_MXU)
    acc = jnp.zeros(x.shape, _F32)
    for c in range(n_chunks):
        gate = _dot(xb, win_ref[0, c])
        up = _dot(xb, win_ref[1, c])
        act = gate * _sigmoid(gate) * up
        acc = acc + _dot(act, wout_ref[c])
    o_ref[...] = x + 0.5 * acc


def _ffn(h2, g, w_in, w_out):
    rows, d = h2.shape
    tm = 512 if rows % 512 == 0 else SEQ_ALIGN
    n_chunks = D_FF // FF_CHUNK
    win = w_in.astype(_MXU).reshape(d, 2, n_chunks, FF_CHUNK).transpose(1, 2, 0, 3)
    wout = w_out.astype(_MXU).reshape(n_chunks, FF_CHUNK, d)
    return pl.pallas_call(
        functools.partial(_ffn_body, n_chunks=n_chunks),
        out_shape=jax.ShapeDtypeStruct((rows, d), _F32),
        grid=(rows // tm,),
        in_specs=[pl.BlockSpec((tm, d), lambda i: (i, 0)),
                  _const_spec((1, d)),
                  _const_spec(win.shape),
                  _const_spec(wout.shape)],
        out_specs=pl.BlockSpec((tm, d), lambda i: (i, 0)),
        compiler_params=_params("parallel"),
        name="ffn",
    )(h2, g.reshape(1, d), win, wout)


def _rwkv_prep_body(h_ref, hprev_ref, ng_ref, mix_ref, wr_ref, wk_ref, wv_ref, w1_ref, w2_ref,
                    a1_ref, a2_ref, g1_ref, g2_ref, w0_ref, a0_ref,
                    r_ref, k_ref, v_ref, lw_ref, a_ref, g_ref):
    ng = ng_ref[...]
    hn = _rms(h_ref[0]) * ng
    prev = (_rms(hprev_ref[0]) * ng)[7:8, :]
    prev = jnp.where(pl.program_id(1) == 0, 0.0, prev)
    row = lax.broadcasted_iota(jnp.int32, hn.shape, 0)
    shifted = jnp.where(row == 0, prev, pltpu.roll(hn, 1, 0))
    xx = shifted - hn

    def mixed(j):
        return (hn + xx * mix_ref[j:j + 1, :]).astype(_MXU)

    r_ref[0] = _dot(mixed(0), wr_ref[...])
    k_ref[0] = _dot(mixed(2), wk_ref[...])
    v_ref[0] = _dot(mixed(3), wv_ref[...])
    wl = w0_ref[...] + _dot(jnp.tanh(_dot(mixed(1), w1_ref[...])), w2_ref[...])
    lw_ref[0] = -0.6065306597126334 * _sigmoid(wl)
    a_ref[0] = _sigmoid(a0_ref[...] + _dot(_dot(mixed(4), a1_ref[...]), a2_ref[...]))
    g_ref[0] = _dot(_sigmoid(_dot(mixed(5), g1_ref[...])), g2_ref[...])


def _rwkv_prep(h, ng, mix, w_r, w_k, w_v, w1, w2, a1, a2, g1, g2, w0, a0):
    b, tp, d = h.shape
    tm = SEQ_ALIGN
    ws = [w.astype(_MXU) for w in (w_r, w_k, w_v, w1, w2, a1, a2, g1, g2)]
    act = pl.BlockSpec((1, tm, d), lambda bi, i: (bi, i, 0))
    prev = pl.BlockSpec((1, 8, d), lambda bi, i: (bi, jnp.maximum(i * (tm // 8) - 1, 0), 0))
    out = jax.ShapeDtypeStruct((b, tp, d), _F32)
    return pl.pallas_call(
        _rwkv_prep_body,
        out_shape=[out] * 6,
        grid=(b, tp // tm),
        in_specs=[act, prev, _const_spec((1, d)), _const_spec(mix.shape)]
                 + [_const_spec(w.shape) for w in ws]
                 + [_const_spec((1, d)), _const_spec((1, d))],
        out_specs=[act] * 6,
        compiler_params=_params("parallel", "parallel"),
        name="rwkv_prep",
    )(h, h, ng.reshape(1, d), mix, *ws, w0.reshape(1, d), a0.reshape(1, d))


def _wkv_body(r_ref, k_ref, v_ref, lw_ref, a_ref, kk_ref, ka_ref, rk_ref, lg_ref, lb_ref,
              o_ref, s_ref, *, n_chunks):
    c_len = WKV_CHUNK

    @pl.when(pl.program_id(2) == 0)
    def _():
        s_ref[...] = jnp.zeros(s_ref.shape, _F32)

    lane = lax.broadcasted_iota(jnp.int32, (c_len, LANES), 1)
    row = lax.broadcasted_iota(jnp.int32, (c_len, LANES), 0)
    first = lane < HEAD_DIM
    col = jnp.where(first, lane, lane - HEAD_DIM)
    strict = col < row
    incl = col <= row
    sq_r = lax.broadcasted_iota(jnp.int32, (LANES, LANES), 0)
    sq_c = lax.broadcasted_iota(jnp.int32, (LANES, LANES), 1)
    same_head = (sq_r < HEAD_DIM) == (sq_c < HEAD_DIM)
    eye = sq_r == sq_c
    tri_r = lax.broadcasted_iota(jnp.int32, (c_len, c_len), 0)
    tri_c = lax.broadcasted_iota(jnp.int32, (c_len, c_len), 1)
    tri = jnp.where(tri_c <= tri_r, 1.0, 0.0).astype(_MXU)

    def head_sum(x):
        s0 = jnp.sum(jnp.where(first, x, 0.0), axis=-1, keepdims=True)
        s1 = jnp.sum(jnp.where(first, 0.0, x), axis=-1, keepdims=True)
        return jnp.where(first, s0, s1)

    def blockdiag(x):
        f = first if x.shape[1] == LANES else jnp.concatenate([first] * (x.shape[1] // LANES), axis=1)
        xb = x.astype(_MXU)
        zero = jnp.zeros_like(xb)
        return jnp.concatenate([jnp.where(f, xb, zero), jnp.where(f, zero, xb)], axis=0)

    def apply(p, x):
        return jnp.dot(p.astype(_MXU), blockdiag(x), preferred_element_type=_F32)

    def scores(x, y):
        return lax.dot_general(x.astype(_MXU), blockdiag(y), (((1,), (1,)), ((), ())),
                               preferred_element_type=_F32)

    kk_w = kk_ref[...]
    ka_w = ka_ref[...]
    rk_w = rk_ref[...]
    ln_g = lg_ref[...]
    ln_b = lb_ref[...]
    state = s_ref[...]

    for c in range(n_chunks):
        sl = pl.ds(c * c_len, c_len)
        r = r_ref[0, sl, :]
        k_raw = k_ref[0, sl, :]
        v = v_ref[0, sl, :]
        lw = lw_ref[0, sl, :]
        gate = a_ref[0, sl, :]

        kk = k_raw * kk_w
        kk = kk / jnp.maximum(jnp.sqrt(head_sum(kk * kk)), 1e-12)
        k = k_raw * (1.0 + (gate - 1.0) * ka_w)
        a_vec = -kk
        b_vec = kk * gate

        h3 = _split3(lw)
        cs = (jnp.dot(tri, h3[0], preferred_element_type=_F32)
              + jnp.dot(tri, h3[1], preferred_element_type=_F32)
              + jnp.dot(tri, h3[2], preferred_element_type=_F32))
        cs_prev = cs - lw
        c_mid = cs[c_len // 2 - 1:c_len // 2, :]
        c_end = cs[c_len - 1:c_len, :]
        a_rel = a_vec * jnp.exp(cs_prev - c_mid)
        r_rel = r * jnp.exp(cs - c_mid)
        inv_rel = jnp.exp(c_mid - cs)
        b_rel = b_vec * inv_rel
        k_rel = k * inv_rel
        a_abs = a_vec * jnp.exp(cs_prev)
        r_abs = r * jnp.exp(cs)
        to_end = jnp.exp(c_end - cs)
        b_end = b_vec * to_end
        k_end = k * to_end
        g_end = jnp.exp(c_end)

        zb = blockdiag(b_rel)
        zk = blockdiag(k_rel)
        lhs = jnp.concatenate([a_rel, r_rel], axis=0).astype(_MXU)
        sb = lax.dot_general(lhs, zb, (((1,), (1,)), ((), ())), preferred_element_type=_F32)
        sk = lax.dot_general(lhs, zk, (((1,), (1,)), ((), ())), preferred_element_type=_F32)
        l_ab = jnp.where(strict, sb[:c_len], 0.0)
        l_ak = jnp.where(strict, sk[:c_len], 0.0)
        g_rb = jnp.where(incl, sb[c_len:], 0.0)
        g_rk = jnp.where(incl, sk[c_len:], 0.0)

        x = jnp.concatenate([a_abs, apply(l_ak, v)], axis=1)
        p = l_ab
        n_pow = c_len.bit_length() - 1
        for i in range(n_pow):
            x = x + apply(p, x)
            if i + 1 < n_pow:
                p = apply(p, p)
        w_a = x[:, :LANES]
        u0 = x[:, LANES:]

        gx = apply(g_rb, x)
        q_eff = r_abs + gx[:, :LANES]
        y0 = gx[:, LANES:] + apply(g_rk, v)

        m_mat = jnp.where(same_head, _dot_tn(w_a, b_end), 0.0) + jnp.where(eye, g_end, 0.0)
        n_mat = jnp.where(same_head, _dot_tn(u0, b_end) + _dot_tn(v, k_end), 0.0)

        s_hi, s_lo = _split2(state)
        y = y0 + _dot_nt(q_eff, s_hi) + _dot_nt(q_eff, s_lo)
        m_hi, m_lo = _split2(m_mat)
        state = (jnp.dot(s_hi, m_hi, preferred_element_type=_F32)
                 + jnp.dot(s_hi, m_lo, preferred_element_type=_F32)
                 + jnp.dot(s_lo, m_hi, preferred_element_type=_F32) + n_mat)

        mu = head_sum(y) * (1.0 / HEAD_DIM)
        yc = y - mu
        var = head_sum(yc * yc) * (1.0 / HEAD_DIM)
        yn = yc * lax.rsqrt(var + RWKV_LN_EPS) * ln_g + ln_b
        o_ref[0, sl, :] = yn + head_sum(r * k * rk_w) * v

    s_ref[...] = state


def _wkv(r, k, v, lw, a, k_k, k_a, r_k, ln_g, ln_b):
    b, tp, d = r.shape
    tc = WKV_CHUNK * WKV_CHUNKS_PER_STEP
    act = pl.BlockSpec((1, tc, LANES), lambda bi, p, s: (bi, s, p))
    par = pl.BlockSpec((1, LANES), lambda bi, p, s: (0, p))
    return pl.pallas_call(
        functools.partial(_wkv_body, n_chunks=WKV_CHUNKS_PER_STEP),
        out_shape=jax.ShapeDtypeStruct((b, tp, d), _F32),
        grid=(b, d // LANES, tp // tc),
        in_specs=[act] * 5 + [par] * 5,
        out_specs=act,
        scratch_shapes=[pltpu.VMEM((LANES, LANES), _F32)],
        compiler_params=_params("parallel", "parallel", "arbitrary"),
        name="wkv",
    )(r, k, v, lw, a, *(p.reshape(1, d) for p in (k_k, k_a, r_k, ln_g, ln_b)))


def _gated_out_body(h_ref, y_ref, g_ref, w_ref, o_ref):
    o_ref[...] = h_ref[...] + _dot(y_ref[...] * g_ref[...], w_ref[...])


def _plain_out_body(h_ref, y_ref, w_ref, o_ref):
    o_ref[...] = h_ref[...] + _dot(y_ref[...], w_ref[...])


def _out_proj(h2, y2, w, gate2=None):
    rows, d = h2.shape
    tm = 512 if rows % 512 == 0 else SEQ_ALIGN
    tile = pl.BlockSpec((tm, d), lambda i: (i, 0))
    acts = [h2, y2] if gate2 is None else [h2, y2, gate2]
    return pl.pallas_call(
        _plain_out_body if gate2 is None else _gated_out_body,
        out_shape=jax.ShapeDtypeStruct((rows, d), _F32),
        grid=(rows // tm,),
        in_specs=[tile] * len(acts) + [_const_spec(w.shape)],
        out_specs=tile,
        compiler_params=_params("parallel"),
        name="out_proj",
    )(*acts, w.astype(_MXU))


def _rwkv_layer(h, ng, p):
    b, tp, d = h.shape
    r, k, v, lw, a, g = _rwkv_prep(h, ng, p["mix"], p["w_r"], p["w_k"], p["w_v"], p["w1"], p["w2"],
                                   p["a1"], p["a2"], p["g1"], p["g2"], p["w0"], p["a0"])
    y = _wkv(r, k, v, lw, a, p["k_k"], p["k_a"], p["r_k"], p["lnx_g"], p["lnx_b"])
    flat = lambda t: t.reshape(b * tp, d)
    return _out_proj(flat(h), flat(y), p["w_o"], flat(g)).reshape(b, tp, d)


def _rope_tables(tp):
    half = ROPE_DIM // 2
    inv = ROPE_THETA ** (-jnp.arange(0, ROPE_DIM, 2, dtype=_F32) / ROPE_DIM)
    ang = jnp.arange(tp, dtype=_F32)[:, None] * inv[None, :]
    cos, sin = jnp.cos(ang), jnp.sin(ang)
    ones = jnp.ones((tp, HEAD_DIM - ROPE_DIM), _F32)
    zeros = jnp.zeros((tp, HEAD_DIM - ROPE_DIM), _F32)
    zh = jnp.zeros((tp, half), _F32)
    c = jnp.concatenate([cos, cos, ones], axis=1)
    s_up = jnp.concatenate([-sin, zh, zeros], axis=1)
    s_dn = jnp.concatenate([zh, sin, zeros], axis=1)
    return tuple(jnp.concatenate([t, t], axis=1) for t in (c, s_up, s_dn))


def _dsa_proj_body(h_ref, ng_ref, wq_ref, wk_ref, wv_ref, wqi_ref, ws_ref, qg_ref, kg_ref,
                   c_ref, su_ref, sd_ref, q_ref, k_ref, v_ref, qi_ref, ki_ref, wi_ref):
    hb = (_rms(h_ref[0]) * ng_ref[...]).astype(_MXU)
    c, s_up, s_dn = c_ref[...], su_ref[...], sd_ref[...]
    lane = lax.broadcasted_iota(jnp.int32, c.shape, 1)
    first = lane < HEAD_DIM

    def rope(x):
        return x * c + pltpu.roll(x, LANES - ROPE_DIM // 2, 1) * s_up + pltpu.roll(x, ROPE_DIM // 2, 1) * s_dn

    def head_rms(x, gain):
        sq = x * x
        m0 = jnp.sum(jnp.where(first, sq, 0.0), axis=-1, keepdims=True) * (1.0 / HEAD_DIM)
        m1 = jnp.sum(jnp.where(first, 0.0, sq), axis=-1, keepdims=True) * (1.0 / HEAD_DIM)
        return x * jnp.where(first, lax.rsqrt(m0 + NORM_EPS), lax.rsqrt(m1 + NORM_EPS)) * gain

    def put_heads(dst_ref, pair, x):
        dst_ref[0, 2 * pair] = x[:, :HEAD_DIM].astype(dst_ref.dtype)
        dst_ref[0, 2 * pair + 1] = x[:, HEAD_DIM:].astype(dst_ref.dtype)

    q = _dot(hb, wq_ref[...])
    for p in range(ATT_Q_W // LANES):
        blk = rope(head_rms(q[:, p * LANES:(p + 1) * LANES], qg_ref[...]))
        put_heads(q_ref, p, blk * (HEAD_DIM ** -0.5))
    k = _dot(hb, wk_ref[...])
    v = _dot(hb, wv_ref[...])
    for p in range(ATT_KV_W // LANES):
        put_heads(k_ref, p, rope(head_rms(k[:, p * LANES:(p + 1) * LANES], kg_ref[...])))
        put_heads(v_ref, p, v[:, p * LANES:(p + 1) * LANES])
    qi = _dot(hb, wqi_ref[...])
    for p in range(IDX_Q_W // LANES):
        put_heads(qi_ref, p, rope(qi[:, p * LANES:(p + 1) * LANES]))
    small = _dot(hb, ws_ref[...])
    ki_ref[0] = rope(small)[:, :IDX_DIM].astype(ki_ref.dtype)
    wi_ref[0] = small[:, IDX_DIM:IDX_DIM + IDX_HEADS] * (IDX_HEADS ** -0.5 * IDX_DIM ** -0.5)


def _dsa_proj(h, ng, w_in, q_g, k_g):
    b, tp, d = h.shape
    tm = SEQ_ALIGN
    o0, o1, o2, o3, o4 = (0, ATT_Q_W, ATT_Q_W + ATT_KV_W, ATT_Q_W + 2 * ATT_KV_W,
                          ATT_Q_W + 2 * ATT_KV_W + IDX_Q_W)
    w = w_in.astype(_MXU)
    w_small = jnp.pad(w[:, o4:], ((0, 0), (0, LANES - (w.shape[1] - o4))))
    ws = [w[:, o0:o1], w[:, o1:o2], w[:, o2:o3], w[:, o3:o4], w_small]
    tables = _rope_tables(tp)
    pair = lambda gvec: jnp.concatenate([gvec, gvec]).reshape(1, LANES)
    heads = lambda n: (jax.ShapeDtypeStruct((b, n, tp, HEAD_DIM), _MXU),
                       pl.BlockSpec((1, n, tm, HEAD_DIM), lambda bi, i: (bi, 0, i, 0)))
    outs = [heads(ATT_HEADS), heads(ATT_KV_HEADS), heads(ATT_KV_HEADS), heads(IDX_HEADS),
            (jax.ShapeDtypeStruct((b, tp, IDX_DIM), _MXU), pl.BlockSpec((1, tm, IDX_DIM), lambda bi, i: (bi, i, 0))),
            (jax.ShapeDtypeStruct((b, tp, IDX_HEADS), _F32), pl.BlockSpec((1, tm, IDX_HEADS), lambda bi, i: (bi, i, 0)))]
    tab = pl.BlockSpec((tm, LANES), lambda bi, i: (i, 0))
    return pl.pallas_call(
        _dsa_proj_body,
        out_shape=[o[0] for o in outs],
        grid=(b, tp // tm),
        in_specs=[pl.BlockSpec((1, tm, d), lambda bi, i: (bi, i, 0)), _const_spec((1, d))]
                 + [_const_spec(x.shape) for x in ws]
                 + [_const_spec((1, LANES)), _const_spec((1, LANES)), tab, tab, tab],
        out_specs=[o[1] for o in outs],
        compiler_params=_params("parallel", "parallel"),
        name="dsa_proj",
    )(h, ng.reshape(1, d), *ws, pair(q_g), pair(k_g), *tables)


def _dsa_attn_body(qi_ref, wi_ref, ki_ref, q_ref, k_ref, v_ref, o_ref,
                   keys_ref, bias_ref, s_ref, wb_ref, *, topk, widths):
    tq = SEQ_ALIGN
    tk = SEQ_ALIGN
    i = pl.program_id(1)
    n_tiles = i + 1

    keys_ref[...] = jnp.full(keys_ref.shape, INT_MIN, jnp.int32)
    wi = wi_ref[0]
    for hd in range(IDX_HEADS):
        wb_ref[hd] = jnp.broadcast_to(wi[:, hd:hd + 1], (tq, LANES))
    qi = qi_ref[0].reshape(IDX_HEADS * tq, IDX_DIM)
    row = lax.broadcasted_iota(jnp.int32, (tq, tk), 0)
    col = lax.broadcasted_iota(jnp.int32, (tq, tk), 1)

    def score_tile(j, carry):
        start = pl.multiple_of(j * tk, tk)
        z = _dot_nt(qi, ki_ref[0, pl.ds(start, tk), :])
        score = jnp.zeros((tq, tk), _F32)
        for hd in range(IDX_HEADS):
            wb = wb_ref[hd]
            score = score + jnp.maximum(z[hd * tq:(hd + 1) * tq], 0.0) * jnp.concatenate([wb, wb], axis=1)
        bits = pltpu.bitcast(score, jnp.int32)
        key = bits ^ ((bits >> 31) & 0x7FFFFFFF)
        key = jnp.maximum(key, INT_MIN + 1)
        causal = (start + col) <= (i * tq + row)
        keys_ref[:, pl.ds(start, tk)] = jnp.where(causal, key, INT_MIN)
        return carry

    lax.fori_loop(0, n_tiles, score_tile, 0)

    def select(width):
        kv = keys_ref[:, :width]

        def count(mask):
            return jnp.sum(mask.astype(jnp.int32), axis=-1, keepdims=True)

        def value_bit(it, thr):
            cand = thr + (jnp.int32(1) << (31 - it))
            return jnp.where(count(kv >= cand) >= topk, cand, thr)

        thr = lax.fori_loop(0, 32, value_bit, jnp.full((tq, 1), INT_MIN, jnp.int32))
        above = kv > thr
        equal = (kv == thr) & (thr > INT_MIN)
        need = topk - count(above)
        pos = lax.broadcasted_iota(jnp.int32, (tq, width), 1)

        def tie_limit():
            def pos_bit(it, last_short):
                cand = last_short + (jnp.int32(1) << (12 - it))
                return jnp.where(count(equal & (pos <= cand)) < need, cand, last_short)
            return lax.fori_loop(0, 13, pos_bit, jnp.full((tq, 1), -1, jnp.int32)) + 1

        has_ties = jnp.max(count(equal) - need) > 0
        limit = lax.cond(has_ties, tie_limit, lambda: jnp.full((tq, 1), width, jnp.int32))
        chosen = above | (equal & (pos <= limit))
        bias_ref[:, :width] = jnp.where(chosen, 0.0, NEG_BIG)

    lo = 0
    for n_q, width in widths:
        @pl.when((i >= lo) & (i < n_q))
        def _(width=width):
            select(width)
        lo = n_q

    for g in range(ATT_KV_HEADS):
        qs = q_ref[0, g * ATT_GROUP:(g + 1) * ATT_GROUP].reshape(ATT_GROUP * tq, HEAD_DIM)

        def logits_tile(j, mx, g=g, qs=qs):
            start = pl.multiple_of(j * tk, tk)
            s = _dot_nt(qs, k_ref[0, g, pl.ds(start, tk), :])
            s = (s.reshape(ATT_GROUP, tq, tk) + bias_ref[:, pl.ds(start, tk)][None]).reshape(ATT_GROUP * tq, tk)
            s_ref[:, pl.ds(start, tk)] = s
            return jnp.maximum(mx, jnp.maximum(s[:, :LANES], s[:, LANES:]))

        mx = lax.fori_loop(0, n_tiles, logits_tile, jnp.full((ATT_GROUP * tq, LANES), NEG_BIG, _F32))
        m = jnp.max(mx, axis=-1, keepdims=True)

        def weighted_tile(j, carry, g=g, m=m):
            den, acc = carry
            start = pl.multiple_of(j * tk, tk)
            p = jnp.exp(s_ref[:, pl.ds(start, tk)] - m)
            den = den + p[:, :LANES] + p[:, LANES:]
            return den, acc + _dot(p, v_ref[0, g, pl.ds(start, tk), :])

        den, acc = lax.fori_loop(0, n_tiles, weighted_tile,
                                 (jnp.zeros((ATT_GROUP * tq, LANES), _F32),
                                  jnp.zeros((ATT_GROUP * tq, HEAD_DIM), _F32)))
        out = (acc / jnp.sum(den, axis=-1, keepdims=True)).reshape(ATT_GROUP, tq, HEAD_DIM)
        w = ATT_GROUP * HEAD_DIM
        o_ref[0, :, g * w:(g + 1) * w] = jnp.concatenate([out[r] for r in range(ATT_GROUP)], axis=1)


def _dsa_attn(q, k, v, qi, ki, wi, topk):
    b, _, tp, _ = q.shape
    tq = SEQ_ALIGN
    nq = tp // tq
    widths = sorted({-(-nq * c // 4) for c in range(1, 5)})
    widths = tuple((n, n * tq) for n in widths)
    per_q = lambda n: pl.BlockSpec((1, n, tq, HEAD_DIM), lambda bi, i: (bi, 0, i, 0))
    per_b = lambda n: pl.BlockSpec((1, n, tp, HEAD_DIM), lambda bi, i: (bi, 0, 0, 0),
                                   pipeline_mode=pl.Buffered(1))
    return pl.pallas_call(
        functools.partial(_dsa_attn_body, topk=topk, widths=widths),
        out_shape=jax.ShapeDtypeStruct((b, tp, ATT_Q_W), _F32),
        grid=(b, nq),
        in_specs=[per_q(IDX_HEADS),
                  pl.BlockSpec((1, tq, IDX_HEADS), lambda bi, i: (bi, i, 0)),
                  pl.BlockSpec((1, tp, IDX_DIM), lambda bi, i: (bi, 0, 0), pipeline_mode=pl.Buffered(1)),
                  per_q(ATT_HEADS), per_b(ATT_KV_HEADS), per_b(ATT_KV_HEADS)],
        out_specs=pl.BlockSpec((1, tq, ATT_Q_W), lambda bi, i: (bi, i, 0)),
        scratch_shapes=[pltpu.VMEM((tq, tp), jnp.int32),
                        pltpu.VMEM((tq, tp), _F32),
                        pltpu.VMEM((ATT_GROUP * tq, tp), _F32),
                        pltpu.VMEM((IDX_HEADS, tq, LANES), _F32)],
        compiler_params=_params("parallel", "arbitrary"),
        name="dsa_attn",
    )(qi, wi, ki, q, k, v)


def _dsa_layer(h, ng, w_in, q_g, k_g, w_o, topk):
    b, tp, d = h.shape
    q, k, v, qi, ki, wi = _dsa_proj(h, ng, w_in, q_g, k_g)
    y = _dsa_attn(q, k, v, qi, ki, wi, topk)
    return _out_proj(h.reshape(b * tp, d), y.reshape(b * tp, d), w_o).reshape(b, tp, d)


def kernel(x, meta, norm_g, ffn_w_in, ffn_w_out, rk_mix, rk_w_r, rk_w_k, rk_w_v, rk_w_o, rk_w0, rk_w1,
           rk_w2, rk_a0, rk_a1, rk_a2, rk_g1, rk_g2, rk_k_k, rk_k_a, rk_r_k, rk_lnx_g, rk_lnx_b,
           at_w_in, at_q_g, at_k_g, at_w_o):
    b, seq, d = x.shape
    topk = min(TOPK_MAX, seq // 4)
    t = seq + N_META
    tp = -(-t // SEQ_ALIGN) * SEQ_ALIGN
    h = jnp.concatenate([jnp.broadcast_to(meta.astype(x.dtype)[None], (b, N_META, d)), x,
                         jnp.zeros((b, tp - t, d), x.dtype)], axis=1)
    rk = dict(mix=rk_mix, w_r=rk_w_r, w_k=rk_w_k, w_v=rk_w_v, w_o=rk_w_o, w0=rk_w0, w1=rk_w1, w2=rk_w2,
              a0=rk_a0, a1=rk_a1, a2=rk_a2, g1=rk_g1, g2=rk_g2, k_k=rk_k_k, k_a=rk_k_a, r_k=rk_r_k,
              lnx_g=rk_lnx_g, lnx_b=rk_lnx_b)
    n_mixers = 2
    for layer in range(norm_g.shape[0]):
        g = norm_g[layer]
        h = _ffn(h.reshape(b * tp, d), g[0], ffn_w_in[layer, 0], ffn_w_out[layer, 0]).reshape(b, tp, d)
        j = layer // n_mixers
        if layer % n_mixers == 0:
            h = _rwkv_layer(h, g[1], {name: val[j] for name, val in rk.items()})
        else:
            h = _dsa_layer(h, g[1], at_w_in[j], at_q_g[j], at_k_g[j], at_w_o[j], topk)
        h = _ffn(h.reshape(b * tp, d), g[2], ffn_w_in[layer, 1], ffn_w_out[layer, 1]).reshape(b, tp, d)
    return h[:, N_META:t]
```

```python
import functools

import jax
import jax.numpy as jnp
from jax import lax
from jax.experimental import pallas as pl
from jax.experimental.pallas import tpu as pltpu

D_MODEL = 1024
N_META = 16
D_FF = 2816
NORM_EPS = 1e-6
HEAD_DIM = 64
ROPE_DIM = HEAD_DIM // 4
ROPE_THETA = 500000.0
RWKV_HEADS = D_MODEL // HEAD_DIM
RWKV_LN_EPS = 64e-5
ATT_HEADS = D_MODEL // HEAD_DIM
ATT_KV_HEADS = 4
ATT_GROUP = ATT_HEADS // ATT_KV_HEADS
IDX_HEADS = 8
IDX_DIM = 64
TOPK_MAX = 256
ATT_Q_W = ATT_HEADS * HEAD_DIM
ATT_KV_W = ATT_KV_HEADS * HEAD_DIM
IDX_Q_W = IDX_HEADS * IDX_DIM

LANES = 128
SEQ_ALIGN = 256
VMEM_LIMIT = 56 * 1024 * 1024
FF_CHUNK = 256
WKV_CHUNK = 64
WKV_CHUNKS_PER_STEP = 4
WKV_PAIRS_PER_STEP = 2
ATT_ROWS = 128
ATT_SPAN = 1024
INT_MIN = -(2 ** 31)
NEG_BIG = -1e30
LOG2_E = 1.4426950408889634

_MXU = jnp.bfloat16
_F32 = jnp.float32


def _dot(a, b):
    return jnp.dot(a.astype(_MXU), b.astype(_MXU), preferred_element_type=_F32)


def _dot_nt(a, b):
    return lax.dot_general(a.astype(_MXU), b.astype(_MXU), (((1,), (1,)), ((), ())),
                           preferred_element_type=_F32)


def _dot_tn(a, b):
    return lax.dot_general(a.astype(_MXU), b.astype(_MXU), (((0,), (0,)), ((), ())),
                           preferred_element_type=_F32)


def _split2(x):
    hi = x.astype(_MXU)
    lo = (x - hi.astype(_F32)).astype(_MXU)
    return hi, lo


def _split3(x):
    hi = x.astype(_MXU)
    r1 = x - hi.astype(_F32)
    mid = r1.astype(_MXU)
    lo = (r1 - mid.astype(_F32)).astype(_MXU)
    return hi, mid, lo


def _rms(x, eps=NORM_EPS):
    return x * lax.rsqrt(jnp.mean(x * x, axis=-1, keepdims=True) + eps)


def _sigmoid(x):
    return 1.0 / (1.0 + jnp.exp(-x))


def _params(*sem):
    return pltpu.CompilerParams(dimension_semantics=sem, vmem_limit_bytes=VMEM_LIMIT)


def _const_spec(shape):
    n = len(shape)
    return pl.BlockSpec(shape, lambda *_: (0,) * n)


def _ffn_body(h_ref, g_ref, win_ref, wout_ref, o_ref, *, n_chunks):
    x = h_ref[...]
    xb = (_rms(x) * g_ref[...]).astype(_MXU)
    acc = jnp.zeros(x.shape, _F32)
    for c in range(n_chunks):
        gate = _dot(xb, win_ref[0, c])
        up = _dot(xb, win_ref[1, c])
        act = gate * _sigmoid(gate) * up
        acc = acc + _dot(act, wout_ref[c])
    o_ref[...] = x + 0.5 * acc


def _ffn(h2, g, w_in, w_out):
    rows, d = h2.shape
    tm = 512 if rows % 512 == 0 else SEQ_ALIGN
    n_chunks = D_FF // FF_CHUNK
    win = w_in.astype(_MXU).reshape(d, 2, n_chunks, FF_CHUNK).transpose(1, 2, 0, 3)
    wout = w_out.astype(_MXU).reshape(n_chunks, FF_CHUNK, d)
    return pl.pallas_call(
        functools.partial(_ffn_body, n_chunks=n_chunks),
        out_shape=jax.ShapeDtypeStruct((rows, d), _F32),
        grid=(rows // tm,),
        in_specs=[pl.BlockSpec((tm, d), lambda i: (i, 0)),
                  _const_spec((1, d)),
                  _const_spec(win.shape),
                  _const_spec(wout.shape)],
        out_specs=pl.BlockSpec((tm, d), lambda i: (i, 0)),
        compiler_params=_params("parallel"),
        name="ffn",
    )(h2, g.reshape(1, d), win, wout)


def _rwkv_prep_body(h_ref, hprev_ref, ng_ref, mix_ref, wr_ref, wk_ref, wv_ref, w1_ref, w2_ref,
                    a1_ref, a2_ref, g1_ref, g2_ref, w0_ref, a0_ref,
                    r_ref, k_ref, v_ref, lw_ref, a_ref, g_ref):
    ng = ng_ref[...]
    hn = _rms(h_ref[0]) * ng
    prev = (_rms(hprev_ref[0]) * ng)[7:8, :]
    prev = jnp.where(pl.program_id(1) == 0, 0.0, prev)
    row = lax.broadcasted_iota(jnp.int32, hn.shape, 0)
    shifted = jnp.where(row == 0, prev, pltpu.roll(hn, 1, 0))
    xx = shifted - hn

    def mixed(j):
        return (hn + xx * mix_ref[j:j + 1, :]).astype(_MXU)

    r_ref[0] = _dot(mixed(0), wr_ref[...])
    k_ref[0] = _dot(mixed(2), wk_ref[...])
    v_ref[0] = _dot(mixed(3), wv_ref[...])
    wl = w0_ref[...] + _dot(jnp.tanh(_dot(mixed(1), w1_ref[...])), w2_ref[...])
    lw_ref[0] = -0.6065306597126334 * _sigmoid(wl)
    a_ref[0] = _sigmoid(a0_ref[...] + _dot(_dot(mixed(4), a1_ref[...]), a2_ref[...]))
    g_ref[0] = _dot(_sigmoid(_dot(mixed(5), g1_ref[...])), g2_ref[...])


def _rwkv_prep(h, ng, mix, w_r, w_k, w_v, w1, w2, a1, a2, g1, g2, w0, a0):
    b, tp, d = h.shape
    tm = SEQ_ALIGN
    ws = [w.astype(_MXU) for w in (w_r, w_k, w_v, w1, w2, a1, a2, g1, g2)]
    act = pl.BlockSpec((1, tm, d), lambda bi, i: (bi, i, 0))
    prev = pl.BlockSpec((1, 8, d), lambda bi, i: (bi, jnp.maximum(i * (tm // 8) - 1, 0), 0))
    out = jax.ShapeDtypeStruct((b, tp, d), _F32)
    return pl.pallas_call(
        _rwkv_prep_body,
        out_shape=[out] * 6,
        grid=(b, tp // tm),
        in_specs=[act, prev, _const_spec((1, d)), _const_spec(mix.shape)]
                 + [_const_spec(w.shape) for w in ws]
                 + [_const_spec((1, d)), _const_spec((1, d))],
        out_specs=[act] * 6,
        compiler_params=_params("parallel", "parallel"),
        name="rwkv_prep",
    )(h, h, ng.reshape(1, d), mix, *ws, w0.reshape(1, d), a0.reshape(1, d))


def _wkv_body(r_ref, k_ref, v_ref, lw_ref, a_ref, kk_ref, ka_ref, rk_ref, lg_ref, lb_ref,
              o_ref, s_ref, *, n_chunks, n_pairs):
    c_len = WKV_CHUNK

    @pl.when(pl.program_id(2) == 0)
    def _():
        s_ref[...] = jnp.zeros(s_ref.shape, _F32)

    lane = lax.broadcasted_iota(jnp.int32, (c_len, LANES), 1)
    row = lax.broadcasted_iota(jnp.int32, (c_len, LANES), 0)
    first = lane < HEAD_DIM
    col = jnp.where(first, lane, lane - HEAD_DIM)
    strict = col < row
    incl = col <= row
    sq_r = lax.broadcasted_iota(jnp.int32, (LANES, LANES), 0)
    sq_c = lax.broadcasted_iota(jnp.int32, (LANES, LANES), 1)
    same_head = (sq_r < HEAD_DIM) == (sq_c < HEAD_DIM)
    eye = sq_r == sq_c
    tri_r = lax.broadcasted_iota(jnp.int32, (c_len, c_len), 0)
    tri_c = lax.broadcasted_iota(jnp.int32, (c_len, c_len), 1)
    tri = jnp.where(tri_c <= tri_r, 1.0, 0.0).astype(_MXU)

    def head_sum(x):
        s0 = jnp.sum(jnp.where(first, x, 0.0), axis=-1, keepdims=True)
        s1 = jnp.sum(jnp.where(first, 0.0, x), axis=-1, keepdims=True)
        return jnp.where(first, s0, s1)

    def blockdiag(x):
        f = first if x.shape[1] == LANES else jnp.concatenate([first] * (x.shape[1] // LANES), axis=1)
        xb = x.astype(_MXU)
        zero = jnp.zeros_like(xb)
        return jnp.concatenate([jnp.where(f, xb, zero), jnp.where(f, zero, xb)], axis=0)

    def apply(p, x):
        return jnp.dot(p.astype(_MXU), blockdiag(x), preferred_element_type=_F32)

    def scores(x, y):
        return lax.dot_general(x.astype(_MXU), blockdiag(y), (((1,), (1,)), ((), ())),
                               preferred_element_type=_F32)

    units = [(c, p) for c in range(n_chunks) for p in range(n_pairs)]
    st = {}
    for u in units:
        c, p = u
        sl = pl.ds(c * c_len, c_len)
        ln = slice(p * LANES, (p + 1) * LANES)
        r = r_ref[0, sl, ln]
        k_raw = k_ref[0, sl, ln]
        v = v_ref[0, sl, ln]
        lw = lw_ref[0, sl, ln]
        gate = a_ref[0, sl, ln]
        kk = k_raw * kk_ref[:, ln]
        kk = kk / jnp.maximum(jnp.sqrt(head_sum(kk * kk)), 1e-12)
        k = k_raw * (1.0 + (gate - 1.0) * ka_ref[:, ln])
        h3 = _split3(lw)
        cs = (jnp.dot(tri, h3[0], preferred_element_type=_F32)
              + jnp.dot(tri, h3[1], preferred_element_type=_F32)
              + jnp.dot(tri, h3[2], preferred_element_type=_F32))
        st[u] = dict(r=r, k=k, v=v, lw=lw, cs=cs, a_vec=-kk, b_vec=kk * gate)

    for u in units:
        d = st[u]
        cs, lw = d["cs"], d["lw"]
        cs_prev = cs - lw
        c_mid = cs[c_len // 2 - 1:c_len // 2, :]
        c_end = cs[c_len - 1:c_len, :]
        inv_rel = jnp.exp(c_mid - cs)
        to_end = jnp.exp(c_end - cs)
        d["a_abs"] = d["a_vec"] * jnp.exp(cs_prev)
        d["r_abs"] = d["r"] * jnp.exp(cs)
        d["b_end"] = d["b_vec"] * to_end
        d["k_end"] = d["k"] * to_end
        d["g_end"] = jnp.exp(c_end)
        lhs = jnp.concatenate([d["a_vec"] * jnp.exp(cs_prev - c_mid), d["r"] * jnp.exp(cs - c_mid)],
                              axis=0).astype(_MXU)
        d["sb"] = lax.dot_general(lhs, blockdiag(d["b_vec"] * inv_rel), (((1,), (1,)), ((), ())),
                                  preferred_element_type=_F32)
        d["sk"] = lax.dot_general(lhs, blockdiag(d["k"] * inv_rel), (((1,), (1,)), ((), ())),
                                  preferred_element_type=_F32)

    for u in units:
        d = st[u]
        d["p"] = jnp.where(strict, d["sb"][:c_len], 0.0)
        d["g_rb"] = jnp.where(incl, d["sb"][c_len:], 0.0)
        d["g_rk"] = jnp.where(incl, d["sk"][c_len:], 0.0)
        l_ak = jnp.where(strict, d["sk"][:c_len], 0.0)
        d["x"] = jnp.concatenate([d["a_abs"], apply(l_ak, d["v"])], axis=1)

    n_pow = c_len.bit_length() - 1
    for i in range(n_pow):
        for u in units:
            d = st[u]
            d["x"] = d["x"] + apply(d["p"], d["x"])
            if i + 1 < n_pow:
                d["p"] = apply(d["p"], d["p"])

    for u in units:
        d = st[u]
        x = d["x"]
        w_a = x[:, :LANES]
        u0 = x[:, LANES:]
        gx = apply(d["g_rb"], x)
        d["q_eff"] = d["r_abs"] + gx[:, :LANES]
        d["y0"] = gx[:, LANES:] + apply(d["g_rk"], d["v"])
        d["m"] = jnp.where(same_head, _dot_tn(w_a, d["b_end"]), 0.0) + jnp.where(eye, d["g_end"], 0.0)
        d["n"] = jnp.where(same_head, _dot_tn(u0, d["b_end"]) + _dot_tn(d["v"], d["k_end"]), 0.0)

    state = [s_ref[p] for p in range(n_pairs)]
    for u in units:
        c, p = u
        d = st[u]
        s_hi, s_lo = _split2(state[p])
        d["y"] = d["y0"] + _dot_nt(d["q_eff"], s_hi) + _dot_nt(d["q_eff"], s_lo)
        m_hi, m_lo = _split2(d["m"])
        state[p] = (jnp.dot(s_hi, m_hi, preferred_element_type=_F32)
                    + jnp.dot(s_hi, m_lo, preferred_element_type=_F32)
                    + jnp.dot(s_lo, m_hi, preferred_element_type=_F32) + d["n"])
    for p in range(n_pairs):
        s_ref[p] = state[p]

    for u in units:
        c, p = u
        d = st[u]
        ln = slice(p * LANES, (p + 1) * LANES)
        y = d["y"]
        mu = head_sum(y) * (1.0 / HEAD_DIM)
        yc = y - mu
        var = head_sum(yc * yc) * (1.0 / HEAD_DIM)
        yn = yc * lax.rsqrt(var + RWKV_LN_EPS) * lg_ref[:, ln] + lb_ref[:, ln]
        o_ref[0, pl.ds(c * c_len, c_len), ln] = yn + head_sum(d["r"] * d["k"] * rk_ref[:, ln]) * d["v"]


def _wkv(r, k, v, lw, a, k_k, k_a, r_k, ln_g, ln_b):
    b, tp, d = r.shape
    tc = WKV_CHUNK * WKV_CHUNKS_PER_STEP
    width = LANES * WKV_PAIRS_PER_STEP
    act = pl.BlockSpec((1, tc, width), lambda bi, p, s: (bi, s, p))
    par = pl.BlockSpec((1, width), lambda bi, p, s: (0, p))
    return pl.pallas_call(
        functools.partial(_wkv_body, n_chunks=WKV_CHUNKS_PER_STEP, n_pairs=WKV_PAIRS_PER_STEP),
        out_shape=jax.ShapeDtypeStruct((b, tp, d), _F32),
        grid=(b, d // width, tp // tc),
        in_specs=[act] * 5 + [par] * 5,
        out_specs=act,
        scratch_shapes=[pltpu.VMEM((WKV_PAIRS_PER_STEP, LANES, LANES), _F32)],
        compiler_params=_params("parallel", "parallel", "arbitrary"),
        name="wkv",
    )(r, k, v, lw, a, *(p.reshape(1, d) for p in (k_k, k_a, r_k, ln_g, ln_b)))


def _gated_out_body(h_ref, y_ref, g_ref, w_ref, o_ref):
    o_ref[...] = h_ref[...] + _dot(y_ref[...] * g_ref[...], w_ref[...])


def _plain_out_body(h_ref, y_ref, w_ref, o_ref):
    o_ref[...] = h_ref[...] + _dot(y_ref[...], w_ref[...])


def _out_proj(h2, y2, w, gate2=None):
    rows, d = h2.shape
    tm = 512 if rows % 512 == 0 else SEQ_ALIGN
    tile = pl.BlockSpec((tm, d), lambda i: (i, 0))
    acts = [h2, y2] if gate2 is None else [h2, y2, gate2]
    return pl.pallas_call(
        _plain_out_body if gate2 is None else _gated_out_body,
        out_shape=jax.ShapeDtypeStruct((rows, d), _F32),
        grid=(rows // tm,),
        in_specs=[tile] * len(acts) + [_const_spec(w.shape)],
        out_specs=tile,
        compiler_params=_params("parallel"),
        name="out_proj",
    )(*acts, w.astype(_MXU))


def _rwkv_layer(h, ng, p):
    b, tp, d = h.shape
    r, k, v, lw, a, g = _rwkv_prep(h, ng, p["mix"], p["w_r"], p["w_k"], p["w_v"], p["w1"], p["w2"],
                                   p["a1"], p["a2"], p["g1"], p["g2"], p["w0"], p["a0"])
    y = _wkv(r, k, v, lw, a, p["k_k"], p["k_a"], p["r_k"], p["lnx_g"], p["lnx_b"])
    flat = lambda t: t.reshape(b * tp, d)
    return _out_proj(flat(h), flat(y), p["w_o"], flat(g)).reshape(b, tp, d)


def _rope_tables(tp):
    half = ROPE_DIM // 2
    inv = ROPE_THETA ** (-jnp.arange(0, ROPE_DIM, 2, dtype=_F32) / ROPE_DIM)
    ang = jnp.arange(tp, dtype=_F32)[:, None] * inv[None, :]
    cos, sin = jnp.cos(ang), jnp.sin(ang)
    ones = jnp.ones((tp, HEAD_DIM - ROPE_DIM), _F32)
    zeros = jnp.zeros((tp, HEAD_DIM - ROPE_DIM), _F32)
    zh = jnp.zeros((tp, half), _F32)
    c = jnp.concatenate([cos, cos, ones], axis=1)
    s_up = jnp.concatenate([-sin, zh, zeros], axis=1)
    s_dn = jnp.concatenate([zh, sin, zeros], axis=1)
    return tuple(jnp.concatenate([t, t], axis=1) for t in (c, s_up, s_dn))


def _dsa_proj_body(h_ref, ng_ref, wq_ref, wk_ref, wv_ref, wqi_ref, ws_ref, qg_ref, kg_ref,
                   c_ref, su_ref, sd_ref, q_ref, k_ref, v_ref, qi_ref, ki_ref, wi_ref):
    hb = (_rms(h_ref[0]) * ng_ref[...]).astype(_MXU)
    c, s_up, s_dn = c_ref[...], su_ref[...], sd_ref[...]
    lane = lax.broadcasted_iota(jnp.int32, c.shape, 1)
    first = lane < HEAD_DIM

    def rope(x):
        return x * c + pltpu.roll(x, LANES - ROPE_DIM // 2, 1) * s_up + pltpu.roll(x, ROPE_DIM // 2, 1) * s_dn

    def head_rms(x, gain):
        sq = x * x
        m0 = jnp.sum(jnp.where(first, sq, 0.0), axis=-1, keepdims=True) * (1.0 / HEAD_DIM)
        m1 = jnp.sum(jnp.where(first, 0.0, sq), axis=-1, keepdims=True) * (1.0 / HEAD_DIM)
        return x * jnp.where(first, lax.rsqrt(m0 + NORM_EPS), lax.rsqrt(m1 + NORM_EPS)) * gain

    def put_heads(dst_ref, pair, x):
        dst_ref[0, 2 * pair] = x[:, :HEAD_DIM].astype(dst_ref.dtype)
        dst_ref[0, 2 * pair + 1] = x[:, HEAD_DIM:].astype(dst_ref.dtype)

    q = _dot(hb, wq_ref[...])
    for p in range(ATT_Q_W // LANES):
        blk = rope(head_rms(q[:, p * LANES:(p + 1) * LANES], qg_ref[...]))
        put_heads(q_ref, p, blk * (HEAD_DIM ** -0.5 * LOG2_E))
    k = _dot(hb, wk_ref[...])
    v = _dot(hb, wv_ref[...])
    for p in range(ATT_KV_W // LANES):
        put_heads(k_ref, p, rope(head_rms(k[:, p * LANES:(p + 1) * LANES], kg_ref[...])))
        put_heads(v_ref, p, v[:, p * LANES:(p + 1) * LANES])
    qi = _dot(hb, wqi_ref[...])
    for p in range(IDX_Q_W // LANES):
        put_heads(qi_ref, p, rope(qi[:, p * LANES:(p + 1) * LANES]))
    small = _dot(hb, ws_ref[...])
    ki_ref[0] = rope(small)[:, :IDX_DIM].astype(ki_ref.dtype)
    wi_ref[0] = small[:, IDX_DIM:IDX_DIM + IDX_HEADS] * (IDX_HEADS ** -0.5 * IDX_DIM ** -0.5)


def _dsa_proj(h, ng, w_in, q_g, k_g):
    b, tp, d = h.shape
    tm = SEQ_ALIGN
    o0, o1, o2, o3, o4 = (0, ATT_Q_W, ATT_Q_W + ATT_KV_W, ATT_Q_W + 2 * ATT_KV_W,
                          ATT_Q_W + 2 * ATT_KV_W + IDX_Q_W)
    w = w_in.astype(_MXU)
    w_small = jnp.pad(w[:, o4:], ((0, 0), (0, LANES - (w.shape[1] - o4))))
    ws = [w[:, o0:o1], w[:, o1:o2], w[:, o2:o3], w[:, o3:o4], w_small]
    tables = _rope_tables(tp)
    pair = lambda gvec: jnp.concatenate([gvec, gvec]).reshape(1, LANES)
    heads = lambda n: (jax.ShapeDtypeStruct((b, n, tp, HEAD_DIM), _MXU),
                       pl.BlockSpec((1, n, tm, HEAD_DIM), lambda bi, i: (bi, 0, i, 0)))
    outs = [heads(ATT_HEADS), heads(ATT_KV_HEADS), heads(ATT_KV_HEADS), heads(IDX_HEADS),
            (jax.ShapeDtypeStruct((b, tp, IDX_DIM), _MXU), pl.BlockSpec((1, tm, IDX_DIM), lambda bi, i: (bi, i, 0))),
            (jax.ShapeDtypeStruct((b, tp, IDX_HEADS), _F32), pl.BlockSpec((1, tm, IDX_HEADS), lambda bi, i: (bi, i, 0)))]
    tab = pl.BlockSpec((tm, LANES), lambda bi, i: (i, 0))
    return pl.pallas_call(
        _dsa_proj_body,
        out_shape=[o[0] for o in outs],
        grid=(b, tp // tm),
        in_specs=[pl.BlockSpec((1, tm, d), lambda bi, i: (bi, i, 0)), _const_spec((1, d))]
                 + [_const_spec(x.shape) for x in ws]
                 + [_const_spec((1, LANES)), _const_spec((1, LANES)), tab, tab, tab],
        out_specs=[o[1] for o in outs],
        compiler_params=_params("parallel", "parallel"),
        name="dsa_proj",
    )(h, ng.reshape(1, d), *ws, pair(q_g), pair(k_g), *tables)


def _dsa_attn_body(qi_ref, wi_ref, ki_ref, q_ref, k_ref, v_ref, o_ref,
                   keys_ref, bias_ref, wb_ref, m_ref, l_ref, acc_ref, *, topk, widths):
    tq = SEQ_ALIGN
    tk = SEQ_ALIGN
    i = pl.program_id(1)
    n_tiles = i + 1

    keys_ref[...] = jnp.full(keys_ref.shape, INT_MIN, jnp.int32)
    wi = wi_ref[0]
    for hd in range(IDX_HEADS):
        wb_ref[hd] = jnp.broadcast_to(wi[:, hd:hd + 1], (tq, LANES))
    qi = qi_ref[0].reshape(IDX_HEADS * tq, IDX_DIM)
    row = lax.broadcasted_iota(jnp.int32, (tq, tk), 0)
    col = lax.broadcasted_iota(jnp.int32, (tq, tk), 1)

    def score_tile(j, carry):
        start = pl.multiple_of(j * tk, tk)
        z = _dot_nt(qi, ki_ref[0, pl.ds(start, tk), :])
        score = jnp.zeros((tq, tk), _F32)
        for hd in range(IDX_HEADS):
            wb = wb_ref[hd]
            score = score + jnp.maximum(z[hd * tq:(hd + 1) * tq], 0.0) * jnp.concatenate([wb, wb], axis=1)
        bits = pltpu.bitcast(score, jnp.int32)
        key = bits ^ ((bits >> 31) & 0x7FFFFFFF)
        key = jnp.maximum(key, INT_MIN + 1)
        causal = (start + col) <= (i * tq + row)
        keys_ref[:, pl.ds(start, tk)] = jnp.where(causal, key, INT_MIN)
        return carry

    lax.fori_loop(0, n_tiles, score_tile, 0)

    def select(width):
        kv = keys_ref[:, :width]

        def count(mask):
            return jnp.sum(mask.astype(jnp.int32), axis=-1, keepdims=True)

        def value_bit(it, thr):
            cand = thr + (jnp.int32(1) << (31 - it))
            return jnp.where(count(kv >= cand) >= topk, cand, thr)

        thr = lax.fori_loop(0, 32, value_bit, jnp.full((tq, 1), INT_MIN, jnp.int32))
        above = kv > thr
        equal = (kv == thr) & (thr > INT_MIN)
        need = topk - count(above)
        pos = lax.broadcasted_iota(jnp.int32, (tq, width), 1)

        def tie_limit():
            def pos_bit(it, last_short):
                cand = last_short + (jnp.int32(1) << (12 - it))
                return jnp.where(count(equal & (pos <= cand)) < need, cand, last_short)
            return lax.fori_loop(0, 13, pos_bit, jnp.full((tq, 1), -1, jnp.int32)) + 1

        has_ties = jnp.max(count(equal) - need) > 0
        limit = lax.cond(has_ties, tie_limit, lambda: jnp.full((tq, 1), width, jnp.int32))
        chosen = above | (equal & (pos <= limit))
        bias_ref[:, :width] = jnp.where(chosen, 0.0, NEG_BIG)

    lo = 0
    for n_q, width in widths:
        @pl.when((i >= lo) & (i < n_q))
        def _(width=width):
            select(width)
        lo = n_q

    rows = ATT_GROUP * ATT_ROWS
    tiles_per_span = ATT_SPAN // tk
    n_spans = n_tiles // tiles_per_span
    left = n_tiles - n_spans * tiles_per_span
    halves = range(tq // ATT_ROWS)

    for g in range(ATT_KV_HEADS):
        heads = slice(g * ATT_GROUP, (g + 1) * ATT_GROUP)
        qs = [q_ref[0, heads, qh * ATT_ROWS:(qh + 1) * ATT_ROWS, :].reshape(rows, HEAD_DIM) for qh in halves]
        m_ref[...] = jnp.full(m_ref.shape, NEG_BIG, _F32)
        l_ref[...] = jnp.zeros(l_ref.shape, _F32)
        acc_ref[...] = jnp.zeros(acc_ref.shape, _F32)

        def flash(start, width, g=g, qs=qs):
            kt = k_ref[0, g, pl.ds(start, width), :]
            vt = v_ref[0, g, pl.ds(start, width), :]
            s = [_dot_nt(qs[qh], kt) for qh in halves]
            p = []
            for qh in halves:
                bias = bias_ref[qh * ATT_ROWS:(qh + 1) * ATT_ROWS, pl.ds(start, width)]
                sq = (s[qh].reshape(ATT_GROUP, ATT_ROWS, width) + bias[None]).reshape(rows, width)
                m_prev = m_ref[qh]
                m_new = jnp.maximum(m_prev, jnp.max(sq, axis=-1, keepdims=True))
                alpha = jnp.exp2(m_prev - m_new)
                pq = jnp.exp2(sq - m_new)
                l_ref[qh] = alpha * l_ref[qh] + jnp.sum(pq, axis=-1, keepdims=True)
                acc_ref[qh] = alpha * acc_ref[qh]
                m_ref[qh] = m_new
                p.append(pq.astype(_MXU))
            for qh in halves:
                acc_ref[qh] += jnp.dot(p[qh], vt, preferred_element_type=_F32)

        def span_step(j, carry, flash=flash):
            flash(pl.multiple_of(j * ATT_SPAN, ATT_SPAN), ATT_SPAN)
            return carry

        lax.fori_loop(0, n_spans, span_step, 0)
        for extra in range(1, tiles_per_span):
            @pl.when(left == extra)
            def _(extra=extra, flash=flash):
                flash(pl.multiple_of(n_spans * ATT_SPAN, ATT_SPAN), extra * tk)
        for qh in halves:
            out = acc_ref[qh] / l_ref[qh]
            o_ref[0, heads, qh * ATT_ROWS:(qh + 1) * ATT_ROWS, :] = (
                out.reshape(ATT_GROUP, ATT_ROWS, HEAD_DIM).astype(o_ref.dtype))


def _dsa_attn(q, k, v, qi, ki, wi, topk):
    b, _, tp, _ = q.shape
    tq = SEQ_ALIGN
    nq = tp // tq
    widths = sorted({-(-nq * c // 4) for c in range(1, 5)})
    widths = tuple((n, n * tq) for n in widths)
    per_q = lambda n: pl.BlockSpec((1, n, tq, HEAD_DIM), lambda bi, i: (bi, 0, i, 0))
    per_b = lambda n: pl.BlockSpec((1, n, tp, HEAD_DIM), lambda bi, i: (bi, 0, 0, 0),
                                   pipeline_mode=pl.Buffered(1))
    return pl.pallas_call(
        functools.partial(_dsa_attn_body, topk=topk, widths=widths),
        out_shape=jax.ShapeDtypeStruct((b, ATT_HEADS, tp, HEAD_DIM), _MXU),
        grid=(b, nq),
        in_specs=[per_q(IDX_HEADS),
                  pl.BlockSpec((1, tq, IDX_HEADS), lambda bi, i: (bi, i, 0)),
                  pl.BlockSpec((1, tp, IDX_DIM), lambda bi, i: (bi, 0, 0), pipeline_mode=pl.Buffered(1)),
                  per_q(ATT_HEADS), per_b(ATT_KV_HEADS), per_b(ATT_KV_HEADS)],
        out_specs=per_q(ATT_HEADS),
        scratch_shapes=[pltpu.VMEM((tq, tp), jnp.int32),
                        pltpu.VMEM((tq, tp), _F32),
                        pltpu.VMEM((IDX_HEADS, tq, LANES), _F32),
                        pltpu.VMEM((tq // ATT_ROWS, ATT_GROUP * ATT_ROWS, 1), _F32),
                        pltpu.VMEM((tq // ATT_ROWS, ATT_GROUP * ATT_ROWS, 1), _F32),
                        pltpu.VMEM((tq // ATT_ROWS, ATT_GROUP * ATT_ROWS, HEAD_DIM), _F32)],
        compiler_params=_params("parallel", "arbitrary"),
        name="dsa_attn",
    )(qi, wi, ki, q, k, v)


def _dsa_out_body(h_ref, y_ref, w_ref, o_ref):
    y = jnp.concatenate([y_ref[0, hd] for hd in range(ATT_HEADS)], axis=1)
    o_ref[0] = h_ref[0] + _dot(y, w_ref[...])


def _dsa_out(h, y, w_o):
    b, tp, d = h.shape
    tm = SEQ_ALIGN
    tile = pl.BlockSpec((1, tm, d), lambda bi, i: (bi, i, 0))
    return pl.pallas_call(
        _dsa_out_body,
        out_shape=jax.ShapeDtypeStruct((b, tp, d), _F32),
        grid=(b, tp // tm),
        in_specs=[tile, pl.BlockSpec((1, ATT_HEADS, tm, HEAD_DIM), lambda bi, i: (bi, 0, i, 0)),
                  _const_spec(w_o.shape)],
        out_specs=tile,
        compiler_params=_params("parallel", "parallel"),
        name="dsa_out",
    )(h, y, w_o.astype(_MXU))


def _dsa_layer(h, ng, w_in, q_g, k_g, w_o, topk):
    q, k, v, qi, ki, wi = _dsa_proj(h, ng, w_in, q_g, k_g)
    return _dsa_out(h, _dsa_attn(q, k, v, qi, ki, wi, topk), w_o)


def kernel(x, meta, norm_g, ffn_w_in, ffn_w_out, rk_mix, rk_w_r, rk_w_k, rk_w_v, rk_w_o, rk_w0, rk_w1,
           rk_w2, rk_a0, rk_a1, rk_a2, rk_g1, rk_g2, rk_k_k, rk_k_a, rk_r_k, rk_lnx_g, rk_lnx_b,
           at_w_in, at_q_g, at_k_g, at_w_o):
    b, seq, d = x.shape
    topk = min(TOPK_MAX, seq // 4)
    t = seq + N_META
    tp = -(-t // SEQ_ALIGN) * SEQ_ALIGN
    h = jnp.concatenate([jnp.broadcast_to(meta.astype(x.dtype)[None], (b, N_META, d)), x,
                         jnp.zeros((b, tp - t, d), x.dtype)], axis=1)
    rk = dict(mix=rk_mix, w_r=rk_w_r, w_k=rk_w_k, w_v=rk_w_v, w_o=rk_w_o, w0=rk_w0, w1=rk_w1, w2=rk_w2,
              a0=rk_a0, a1=rk_a1, a2=rk_a2, g1=rk_g1, g2=rk_g2, k_k=rk_k_k, k_a=rk_k_a, r_k=rk_r_k,
              lnx_g=rk_lnx_g, lnx_b=rk_lnx_b)
    n_mixers = 2
    for layer in range(norm_g.shape[0]):
        g = norm_g[layer]
        h = _ffn(h.reshape(b * tp, d), g[0], ffn_w_in[layer, 0], ffn_w_out[layer, 0]).reshape(b, tp, d)
        j = layer // n_mixers
        if layer % n_mixers == 0:
            h = _rwkv_layer(h, g[1], {name: val[j] for name, val in rk.items()})
        else:
            h = _dsa_layer(h, g[1], at_w_in[j], at_q_g[j], at_k_g[j], at_w_o[j], topk)
        h = _ffn(h.reshape(b * tp, d), g[2], ffn_w_in[layer, 1], ffn_w_out[layer, 1]).reshape(b, tp, d)
    return h[:, N_META:t]
```

```python
import functools

import jax
import jax.numpy as jnp
from jax import lax
from jax.experimental import pallas as pl
from jax.experimental.pallas import tpu as pltpu

D_MODEL = 1024
N_META = 16
D_FF = 2816
NORM_EPS = 1e-6
HEAD_DIM = 64
ROPE_DIM = HEAD_DIM // 4
ROPE_THETA = 500000.0
RWKV_HEADS = D_MODEL // HEAD_DIM
RWKV_LN_EPS = 64e-5
ATT_HEADS = D_MODEL // HEAD_DIM
ATT_KV_HEADS = 4
ATT_GROUP = ATT_HEADS // ATT_KV_HEADS
IDX_HEADS = 8
IDX_DIM = 64
TOPK_MAX = 256
ATT_Q_W = ATT_HEADS * HEAD_DIM
ATT_KV_W = ATT_KV_HEADS * HEAD_DIM
IDX_Q_W = IDX_HEADS * IDX_DIM

LANES = 128
SEQ_ALIGN = 256
VMEM_LIMIT = 56 * 1024 * 1024
FF_CHUNK = 256
WKV_CHUNK = 64
WKV_CHUNKS_PER_STEP = 4
WKV_PAIRS_PER_STEP = 2
ATT_ROWS = 128
ATT_SPAN = 1024
ATT_GROUPS_PER_STEP = 2
INT_MIN = -(2 ** 31)
NEG_BIG = -1e30
LOG2_E = 1.4426950408889634

_MXU = jnp.bfloat16
_F32 = jnp.float32


def _dot(a, b):
    return jnp.dot(a.astype(_MXU), b.astype(_MXU), preferred_element_type=_F32)


def _dot_nt(a, b):
    return lax.dot_general(a.astype(_MXU), b.astype(_MXU), (((1,), (1,)), ((), ())),
                           preferred_element_type=_F32)


def _dot_tn(a, b):
    return lax.dot_general(a.astype(_MXU), b.astype(_MXU), (((0,), (0,)), ((), ())),
                           preferred_element_type=_F32)


def _split2(x):
    hi = x.astype(_MXU)
    lo = (x - hi.astype(_F32)).astype(_MXU)
    return hi, lo


def _split3(x):
    hi = x.astype(_MXU)
    r1 = x - hi.astype(_F32)
    mid = r1.astype(_MXU)
    lo = (r1 - mid.astype(_F32)).astype(_MXU)
    return hi, mid, lo


def _rms(x, eps=NORM_EPS):
    return x * lax.rsqrt(jnp.mean(x * x, axis=-1, keepdims=True) + eps)


def _sigmoid(x):
    return 1.0 / (1.0 + jnp.exp(-x))


def _params(*sem):
    return pltpu.CompilerParams(dimension_semantics=sem, vmem_limit_bytes=VMEM_LIMIT)


def _const_spec(shape):
    n = len(shape)
    return pl.BlockSpec(shape, lambda *_: (0,) * n)


def _ffn_body(h_ref, g_ref, win_ref, wout_ref, o_ref, *, n_chunks):
    x = h_ref[...]
    xb = (_rms(x) * g_ref[...]).astype(_MXU)
    acc = jnp.zeros(x.shape, _F32)
    for c in range(n_chunks):
        gate = _dot(xb, win_ref[0, c])
        up = _dot(xb, win_ref[1, c])
        act = gate * _sigmoid(gate) * up
        acc = acc + _dot(act, wout_ref[c])
    o_ref[...] = x + 0.5 * acc


def _ffn(h2, g, w_in, w_out):
    rows, d = h2.shape
    tm = 512 if rows % 512 == 0 else SEQ_ALIGN
    n_chunks = D_FF // FF_CHUNK
    win = w_in.astype(_MXU).reshape(d, 2, n_chunks, FF_CHUNK).transpose(1, 2, 0, 3)
    wout = w_out.astype(_MXU).reshape(n_chunks, FF_CHUNK, d)
    return pl.pallas_call(
        functools.partial(_ffn_body, n_chunks=n_chunks),
        out_shape=jax.ShapeDtypeStruct((rows, d), _F32),
        grid=(rows // tm,),
        in_specs=[pl.BlockSpec((tm, d), lambda i: (i, 0)),
                  _const_spec((1, d)),
                  _const_spec(win.shape),
                  _const_spec(wout.shape)],
        out_specs=pl.BlockSpec((tm, d), lambda i: (i, 0)),
        compiler_params=_params("parallel"),
        name="ffn",
    )(h2, g.reshape(1, d), win, wout)


def _rwkv_prep_body(h_ref, hprev_ref, ng_ref, mix_ref, wr_ref, wk_ref, wv_ref, w1_ref, w2_ref,
                    a1_ref, a2_ref, g1_ref, g2_ref, w0_ref, a0_ref,
                    r_ref, k_ref, v_ref, lw_ref, a_ref, g_ref):
    ng = ng_ref[...]
    hn = _rms(h_ref[0]) * ng
    prev = (_rms(hprev_ref[0]) * ng)[7:8, :]
    prev = jnp.where(pl.program_id(1) == 0, 0.0, prev)
    row = lax.broadcasted_iota(jnp.int32, hn.shape, 0)
    shifted = jnp.where(row == 0, prev, pltpu.roll(hn, 1, 0))
    xx = shifted - hn

    def mixed(j):
        return (hn + xx * mix_ref[j:j + 1, :]).astype(_MXU)

    r_ref[0] = _dot(mixed(0), wr_ref[...])
    k_ref[0] = _dot(mixed(2), wk_ref[...])
    v_ref[0] = _dot(mixed(3), wv_ref[...])
    wl = w0_ref[...] + _dot(jnp.tanh(_dot(mixed(1), w1_ref[...])), w2_ref[...])
    lw_ref[0] = -0.6065306597126334 * _sigmoid(wl)
    a_ref[0] = _sigmoid(a0_ref[...] + _dot(_dot(mixed(4), a1_ref[...]), a2_ref[...]))
    g_ref[0] = _dot(_sigmoid(_dot(mixed(5), g1_ref[...])), g2_ref[...])


def _rwkv_prep(h, ng, mix, w_r, w_k, w_v, w1, w2, a1, a2, g1, g2, w0, a0):
    b, tp, d = h.shape
    tm = SEQ_ALIGN
    ws = [w.astype(_MXU) for w in (w_r, w_k, w_v, w1, w2, a1, a2, g1, g2)]
    act = pl.BlockSpec((1, tm, d), lambda bi, i: (bi, i, 0))
    prev = pl.BlockSpec((1, 8, d), lambda bi, i: (bi, jnp.maximum(i * (tm // 8) - 1, 0), 0))
    out = jax.ShapeDtypeStruct((b, tp, d), _F32)
    return pl.pallas_call(
        _rwkv_prep_body,
        out_shape=[out] * 6,
        grid=(b, tp // tm),
        in_specs=[act, prev, _const_spec((1, d)), _const_spec(mix.shape)]
                 + [_const_spec(w.shape) for w in ws]
                 + [_const_spec((1, d)), _const_spec((1, d))],
        out_specs=[act] * 6,
        compiler_params=_params("parallel", "parallel"),
        name="rwkv_prep",
    )(h, h, ng.reshape(1, d), mix, *ws, w0.reshape(1, d), a0.reshape(1, d))


def _wkv_body(r_ref, k_ref, v_ref, lw_ref, a_ref, kk_ref, ka_ref, rk_ref, lg_ref, lb_ref,
              o_ref, s_ref, *, n_chunks, n_pairs):
    c_len = WKV_CHUNK

    @pl.when(pl.program_id(2) == 0)
    def _():
        s_ref[...] = jnp.zeros(s_ref.shape, _F32)

    lane = lax.broadcasted_iota(jnp.int32, (c_len, LANES), 1)
    row = lax.broadcasted_iota(jnp.int32, (c_len, LANES), 0)
    first = lane < HEAD_DIM
    col = jnp.where(first, lane, lane - HEAD_DIM)
    strict = col < row
    incl = col <= row
    sq_r = lax.broadcasted_iota(jnp.int32, (LANES, LANES), 0)
    sq_c = lax.broadcasted_iota(jnp.int32, (LANES, LANES), 1)
    same_head = (sq_r < HEAD_DIM) == (sq_c < HEAD_DIM)
    eye = sq_r == sq_c
    tri_r = lax.broadcasted_iota(jnp.int32, (c_len, c_len), 0)
    tri_c = lax.broadcasted_iota(jnp.int32, (c_len, c_len), 1)
    tri = jnp.where(tri_c <= tri_r, 1.0, 0.0).astype(_MXU)

    def head_sum(x):
        s0 = jnp.sum(jnp.where(first, x, 0.0), axis=-1, keepdims=True)
        s1 = jnp.sum(jnp.where(first, 0.0, x), axis=-1, keepdims=True)
        return jnp.where(first, s0, s1)

    def blockdiag(x):
        f = first if x.shape[1] == LANES else jnp.concatenate([first] * (x.shape[1] // LANES), axis=1)
        xb = x.astype(_MXU)
        zero = jnp.zeros_like(xb)
        return jnp.concatenate([jnp.where(f, xb, zero), jnp.where(f, zero, xb)], axis=0)

    def apply(p, x):
        return jnp.dot(p.astype(_MXU), blockdiag(x), preferred_element_type=_F32)

    units = [(c, p) for c in range(n_chunks) for p in range(n_pairs)]
    st = {}
    for u in units:
        c, p = u
        sl = pl.ds(c * c_len, c_len)
        ln = slice(p * LANES, (p + 1) * LANES)
        r = r_ref[0, sl, ln]
        k_raw = k_ref[0, sl, ln]
        v = v_ref[0, sl, ln]
        lw = lw_ref[0, sl, ln]
        gate = a_ref[0, sl, ln]
        kk = k_raw * kk_ref[:, ln]
        kk = kk / jnp.maximum(jnp.sqrt(head_sum(kk * kk)), 1e-12)
        k = k_raw * (1.0 + (gate - 1.0) * ka_ref[:, ln])
        h3 = _split3(lw)
        cs = (jnp.dot(tri, h3[0], preferred_element_type=_F32)
              + jnp.dot(tri, h3[1], preferred_element_type=_F32)
              + jnp.dot(tri, h3[2], preferred_element_type=_F32))
        st[u] = dict(r=r, k=k, v=v, lw=lw, cs=cs, a_vec=-kk, b_vec=kk * gate)

    for u in units:
        d = st[u]
        cs, lw = d["cs"], d["lw"]
        cs_prev = cs - lw
        c_mid = cs[c_len // 2 - 1:c_len // 2, :]
        c_end = cs[c_len - 1:c_len, :]
        inv_rel = jnp.exp(c_mid - cs)
        to_end = jnp.exp(c_end - cs)
        d["a_abs"] = d["a_vec"] * jnp.exp(cs_prev)
        d["r_abs"] = d["r"] * jnp.exp(cs)
        d["b_end"] = d["b_vec"] * to_end
        d["k_end"] = d["k"] * to_end
        d["g_end"] = jnp.exp(c_end)
        lhs = jnp.concatenate([d["a_vec"] * jnp.exp(cs_prev - c_mid), d["r"] * jnp.exp(cs - c_mid)],
                              axis=0).astype(_MXU)
        d["sb"] = lax.dot_general(lhs, blockdiag(d["b_vec"] * inv_rel), (((1,), (1,)), ((), ())),
                                  preferred_element_type=_F32)
        d["sk"] = lax.dot_general(lhs, blockdiag(d["k"] * inv_rel), (((1,), (1,)), ((), ())),
                                  preferred_element_type=_F32)

    for u in units:
        d = st[u]
        d["p"] = jnp.where(strict, d["sb"][:c_len], 0.0)
        d["g_rb"] = jnp.where(incl, d["sb"][c_len:], 0.0)
        d["g_rk"] = jnp.where(incl, d["sk"][c_len:], 0.0)
        l_ak = jnp.where(strict, d["sk"][:c_len], 0.0)
        d["x"] = jnp.concatenate([d["a_abs"], apply(l_ak, d["v"])], axis=1)

    n_pow = c_len.bit_length() - 1
    for i in range(n_pow):
        for u in units:
            d = st[u]
            d["x"] = d["x"] + apply(d["p"], d["x"])
            if i + 1 < n_pow:
                d["p"] = apply(d["p"], d["p"])

    for u in units:
        d = st[u]
        x = d["x"]
        w_a = x[:, :LANES]
        u0 = x[:, LANES:]
        gx = apply(d["g_rb"], x)
        d["q_eff"] = d["r_abs"] + gx[:, :LANES]
        d["y0"] = gx[:, LANES:] + apply(d["g_rk"], d["v"])
        d["m"] = jnp.where(same_head, _dot_tn(w_a, d["b_end"]), 0.0) + jnp.where(eye, d["g_end"], 0.0)
        d["n"] = jnp.where(same_head, _dot_tn(u0, d["b_end"]) + _dot_tn(d["v"], d["k_end"]), 0.0)

    state = [s_ref[p] for p in range(n_pairs)]
    for u in units:
        c, p = u
        d = st[u]
        s_hi, s_lo = _split2(state[p])
        d["y"] = d["y0"] + _dot_nt(d["q_eff"], s_hi) + _dot_nt(d["q_eff"], s_lo)
        m_hi, m_lo = _split2(d["m"])
        state[p] = (jnp.dot(s_hi, m_hi, preferred_element_type=_F32)
                    + jnp.dot(s_hi, m_lo, preferred_element_type=_F32)
                    + jnp.dot(s_lo, m_hi, preferred_element_type=_F32) + d["n"])
    for p in range(n_pairs):
        s_ref[p] = state[p]

    for u in units:
        c, p = u
        d = st[u]
        ln = slice(p * LANES, (p + 1) * LANES)
        y = d["y"]
        mu = head_sum(y) * (1.0 / HEAD_DIM)
        yc = y - mu
        var = head_sum(yc * yc) * (1.0 / HEAD_DIM)
        yn = yc * lax.rsqrt(var + RWKV_LN_EPS) * lg_ref[:, ln] + lb_ref[:, ln]
        o_ref[0, pl.ds(c * c_len, c_len), ln] = yn + head_sum(d["r"] * d["k"] * rk_ref[:, ln]) * d["v"]


def _wkv(r, k, v, lw, a, k_k, k_a, r_k, ln_g, ln_b):
    b, tp, d = r.shape
    tc = WKV_CHUNK * WKV_CHUNKS_PER_STEP
    width = LANES * WKV_PAIRS_PER_STEP
    act = pl.BlockSpec((1, tc, width), lambda bi, p, s: (bi, s, p))
    par = pl.BlockSpec((1, width), lambda bi, p, s: (0, p))
    return pl.pallas_call(
        functools.partial(_wkv_body, n_chunks=WKV_CHUNKS_PER_STEP, n_pairs=WKV_PAIRS_PER_STEP),
        out_shape=jax.ShapeDtypeStruct((b, tp, d), _F32),
        grid=(b, d // width, tp // tc),
        in_specs=[act] * 5 + [par] * 5,
        out_specs=act,
        scratch_shapes=[pltpu.VMEM((WKV_PAIRS_PER_STEP, LANES, LANES), _F32)],
        compiler_params=_params("parallel", "parallel", "arbitrary"),
        name="wkv",
    )(r, k, v, lw, a, *(p.reshape(1, d) for p in (k_k, k_a, r_k, ln_g, ln_b)))


def _gated_out_body(h_ref, y_ref, g_ref, w_ref, o_ref):
    o_ref[...] = h_ref[...] + _dot(y_ref[...] * g_ref[...], w_ref[...])


def _gated_out(h2, y2, gate2, w):
    rows, d = h2.shape
    tm = 512 if rows % 512 == 0 else SEQ_ALIGN
    tile = pl.BlockSpec((tm, d), lambda i: (i, 0))
    return pl.pallas_call(
        _gated_out_body,
        out_shape=jax.ShapeDtypeStruct((rows, d), _F32),
        grid=(rows // tm,),
        in_specs=[tile] * 3 + [_const_spec(w.shape)],
        out_specs=tile,
        compiler_params=_params("parallel"),
        name="rwkv_out",
    )(h2, y2, gate2, w.astype(_MXU))


def _rwkv_layer(h, ng, p):
    b, tp, d = h.shape
    r, k, v, lw, a, g = _rwkv_prep(h, ng, p["mix"], p["w_r"], p["w_k"], p["w_v"], p["w1"], p["w2"],
                                   p["a1"], p["a2"], p["g1"], p["g2"], p["w0"], p["a0"])
    y = _wkv(r, k, v, lw, a, p["k_k"], p["k_a"], p["r_k"], p["lnx_g"], p["lnx_b"])
    flat = lambda t: t.reshape(b * tp, d)
    return _gated_out(flat(h), flat(y), flat(g), p["w_o"]).reshape(b, tp, d)


def _rope_tables(tp):
    half = ROPE_DIM // 2
    inv = ROPE_THETA ** (-jnp.arange(0, ROPE_DIM, 2, dtype=_F32) / ROPE_DIM)
    ang = jnp.arange(tp, dtype=_F32)[:, None] * inv[None, :]
    cos, sin = jnp.cos(ang), jnp.sin(ang)
    ones = jnp.ones((tp, HEAD_DIM - ROPE_DIM), _F32)
    zeros = jnp.zeros((tp, HEAD_DIM - ROPE_DIM), _F32)
    zh = jnp.zeros((tp, half), _F32)
    c = jnp.concatenate([cos, cos, ones], axis=1)
    s_up = jnp.concatenate([-sin, zh, zeros], axis=1)
    s_dn = jnp.concatenate([zh, sin, zeros], axis=1)
    return tuple(jnp.concatenate([t, t], axis=1) for t in (c, s_up, s_dn))


def _dsa_proj_body(h_ref, ng_ref, wq_ref, wk_ref, wv_ref, wqi_ref, ws_ref, qg_ref, kg_ref,
                   c_ref, su_ref, sd_ref, q_ref, k_ref, v_ref, qi_ref, ki_ref, wi_ref):
    hb = (_rms(h_ref[0]) * ng_ref[...]).astype(_MXU)
    c, s_up, s_dn = c_ref[...], su_ref[...], sd_ref[...]
    lane = lax.broadcasted_iota(jnp.int32, c.shape, 1)
    first = lane < HEAD_DIM

    def rope(x):
        return x * c + pltpu.roll(x, LANES - ROPE_DIM // 2, 1) * s_up + pltpu.roll(x, ROPE_DIM // 2, 1) * s_dn

    def head_rms(x, gain):
        sq = x * x
        m0 = jnp.sum(jnp.where(first, sq, 0.0), axis=-1, keepdims=True) * (1.0 / HEAD_DIM)
        m1 = jnp.sum(jnp.where(first, 0.0, sq), axis=-1, keepdims=True) * (1.0 / HEAD_DIM)
        return x * jnp.where(first, lax.rsqrt(m0 + NORM_EPS), lax.rsqrt(m1 + NORM_EPS)) * gain

    def put_heads(dst_ref, pair, x):
        dst_ref[0, 2 * pair] = x[:, :HEAD_DIM].astype(dst_ref.dtype)
        dst_ref[0, 2 * pair + 1] = x[:, HEAD_DIM:].astype(dst_ref.dtype)

    def with_ones(x):
        return jnp.where(first, x, jnp.where(lane == HEAD_DIM, 1.0, 0.0)).astype(v_ref.dtype)

    q = _dot(hb, wq_ref[...])
    for p in range(ATT_Q_W // LANES):
        blk = rope(head_rms(q[:, p * LANES:(p + 1) * LANES], qg_ref[...]))
        put_heads(q_ref, p, blk * (HEAD_DIM ** -0.5 * LOG2_E))
    k = _dot(hb, wk_ref[...])
    v = _dot(hb, wv_ref[...])
    for p in range(ATT_KV_W // LANES):
        put_heads(k_ref, p, rope(head_rms(k[:, p * LANES:(p + 1) * LANES], kg_ref[...])))
        vp = v[:, p * LANES:(p + 1) * LANES]
        v_ref[0, 2 * p] = with_ones(vp)
        v_ref[0, 2 * p + 1] = with_ones(pltpu.roll(vp, HEAD_DIM, 1))
    qi = _dot(hb, wqi_ref[...])
    for p in range(IDX_Q_W // LANES):
        put_heads(qi_ref, p, rope(qi[:, p * LANES:(p + 1) * LANES]))
    small = _dot(hb, ws_ref[...])
    ki_ref[0] = rope(small)[:, :IDX_DIM].astype(ki_ref.dtype)
    wi_ref[0] = small[:, IDX_DIM:IDX_DIM + IDX_HEADS] * (IDX_HEADS ** -0.5 * IDX_DIM ** -0.5)


def _dsa_proj(h, ng, w_in, q_g, k_g):
    b, tp, d = h.shape
    tm = SEQ_ALIGN
    o0, o1, o2, o3, o4 = (0, ATT_Q_W, ATT_Q_W + ATT_KV_W, ATT_Q_W + 2 * ATT_KV_W,
                          ATT_Q_W + 2 * ATT_KV_W + IDX_Q_W)
    w = w_in.astype(_MXU)
    w_small = jnp.pad(w[:, o4:], ((0, 0), (0, LANES - (w.shape[1] - o4))))
    ws = [w[:, o0:o1], w[:, o1:o2], w[:, o2:o3], w[:, o3:o4], w_small]
    tables = _rope_tables(tp)
    pair = lambda gvec: jnp.concatenate([gvec, gvec]).reshape(1, LANES)
    heads = lambda n: (jax.ShapeDtypeStruct((b, n, tp, HEAD_DIM), _MXU),
                       pl.BlockSpec((1, n, tm, HEAD_DIM), lambda bi, i: (bi, 0, i, 0)))
    outs = [heads(ATT_HEADS), heads(ATT_KV_HEADS),
            (jax.ShapeDtypeStruct((b, ATT_KV_HEADS, tp, LANES), _MXU),
             pl.BlockSpec((1, ATT_KV_HEADS, tm, LANES), lambda bi, i: (bi, 0, i, 0))),
            heads(IDX_HEADS),
            (jax.ShapeDtypeStruct((b, tp, IDX_DIM), _MXU), pl.BlockSpec((1, tm, IDX_DIM), lambda bi, i: (bi, i, 0))),
            (jax.ShapeDtypeStruct((b, tp, IDX_HEADS), _F32), pl.BlockSpec((1, tm, IDX_HEADS), lambda bi, i: (bi, i, 0)))]
    tab = pl.BlockSpec((tm, LANES), lambda bi, i: (i, 0))
    return pl.pallas_call(
        _dsa_proj_body,
        out_shape=[o[0] for o in outs],
        grid=(b, tp // tm),
        in_specs=[pl.BlockSpec((1, tm, d), lambda bi, i: (bi, i, 0)), _const_spec((1, d))]
                 + [_const_spec(x.shape) for x in ws]
                 + [_const_spec((1, LANES)), _const_spec((1, LANES)), tab, tab, tab],
        out_specs=[o[1] for o in outs],
        compiler_params=_params("parallel", "parallel"),
        name="dsa_proj",
    )(h, ng.reshape(1, d), *ws, pair(q_g), pair(k_g), *tables)


def _dsa_attn_body(qi_ref, wi_ref, ki_ref, q_ref, k_ref, v_ref, o_ref,
                   keys_ref, bias_ref, wb_ref, m_ref, acc_ref, *, topk, widths):
    tq = SEQ_ALIGN
    tk = SEQ_ALIGN
    i = pl.program_id(1)
    n_tiles = i + 1

    keys_ref[...] = jnp.full(keys_ref.shape, INT_MIN, jnp.int32)
    wi = wi_ref[0]
    for hd in range(IDX_HEADS):
        wb_ref[hd] = jnp.broadcast_to(wi[:, hd:hd + 1], (tq, LANES))
    qi = qi_ref[0].reshape(IDX_HEADS * tq, IDX_DIM)
    row = lax.broadcasted_iota(jnp.int32, (tq, tk), 0)
    col = lax.broadcasted_iota(jnp.int32, (tq, tk), 1)

    def score_tile(j, carry):
        start = pl.multiple_of(j * tk, tk)
        z = _dot_nt(qi, ki_ref[0, pl.ds(start, tk), :])
        score = jnp.zeros((tq, tk), _F32)
        for hd in range(IDX_HEADS):
            wb = wb_ref[hd]
            score = score + jnp.maximum(z[hd * tq:(hd + 1) * tq], 0.0) * jnp.concatenate([wb, wb], axis=1)
        bits = pltpu.bitcast(score, jnp.int32)
        key = bits ^ ((bits >> 31) & 0x7FFFFFFF)
        key = jnp.where(bits == INT_MIN, 0, key)
        key = jnp.maximum(key, INT_MIN + 1)
        causal = (start + col) <= (i * tq + row)
        keys_ref[:, pl.ds(start, tk)] = jnp.where(causal, key, INT_MIN)
        return carry

    lax.fori_loop(0, n_tiles, score_tile, 0)

    def select(width):
        kv = keys_ref[:, :width]

        def count(mask):
            return jnp.sum(mask.astype(jnp.int32), axis=-1, keepdims=True)

        def value_bit(it, thr):
            cand = thr + (jnp.int32(1) << (31 - it))
            return jnp.where(count(kv >= cand) >= topk, cand, thr)

        thr = lax.fori_loop(0, 32, value_bit, jnp.full((tq, 1), INT_MIN, jnp.int32))
        above = kv > thr
        equal = (kv == thr) & (thr > INT_MIN)
        need = topk - count(above)
        pos = lax.broadcasted_iota(jnp.int32, (tq, width), 1)

        def tie_limit():
            def pos_bit(it, last_short):
                cand = last_short + (jnp.int32(1) << (12 - it))
                return jnp.where(count(equal & (pos <= cand)) < need, cand, last_short)
            return lax.fori_loop(0, 13, pos_bit, jnp.full((tq, 1), -1, jnp.int32)) + 1

        has_ties = jnp.max(count(equal) - need) > 0
        limit = lax.cond(has_ties, tie_limit, lambda: jnp.full((tq, 1), width, jnp.int32))
        chosen = above | (equal & (pos <= limit))
        bias_ref[:, :width] = jnp.where(chosen, 0.0, NEG_BIG)

    lo = 0
    for n_q, width in widths:
        @pl.when((i >= lo) & (i < n_q))
        def _(width=width):
            select(width)
        lo = n_q

    rows = ATT_GROUP * ATT_ROWS
    tiles_per_span = ATT_SPAN // tk
    n_spans = n_tiles // tiles_per_span
    left = n_tiles - n_spans * tiles_per_span
    chains = [(gl, qh) for gl in range(ATT_GROUPS_PER_STEP) for qh in range(tq // ATT_ROWS)]

    def group_step(gs, carry):
        kv = [gs * ATT_GROUPS_PER_STEP + gl for gl, _ in chains]
        q_rows = [slice(qh * ATT_ROWS, (qh + 1) * ATT_ROWS) for _, qh in chains]
        qs = [q_ref[0, pl.ds(kv[c] * ATT_GROUP, ATT_GROUP), q_rows[c], :].reshape(rows, HEAD_DIM)
              for c in range(len(chains))]
        m_ref[...] = jnp.full(m_ref.shape, NEG_BIG, _F32)
        acc_ref[...] = jnp.zeros(acc_ref.shape, _F32)

        def flash(start, width):
            s = [_dot_nt(qs[c], k_ref[0, kv[c], pl.ds(start, width), :]) for c in range(len(chains))]
            p = []
            for c in range(len(chains)):
                bias = bias_ref[q_rows[c], pl.ds(start, width)]
                sc = (s[c].reshape(ATT_GROUP, ATT_ROWS, width) + bias[None]).reshape(rows, width)
                m_prev = m_ref[c]
                m_new = jnp.maximum(m_prev, jnp.max(sc, axis=-1, keepdims=True))
                acc_ref[c] = jnp.exp2(m_prev - m_new) * acc_ref[c]
                m_ref[c] = m_new
                p.append(jnp.exp2(sc - m_new).astype(_MXU))
            for c in range(len(chains)):
                acc_ref[c] += jnp.dot(p[c], v_ref[0, kv[c], pl.ds(start, width), :],
                                      preferred_element_type=_F32)

        def span_step(j, carry):
            flash(pl.multiple_of(j * ATT_SPAN, ATT_SPAN), ATT_SPAN)
            return carry

        lax.fori_loop(0, n_spans, span_step, 0)
        for extra in range(1, tiles_per_span):
            @pl.when(left == extra)
            def _(extra=extra):
                flash(pl.multiple_of(n_spans * ATT_SPAN, ATT_SPAN), extra * tk)
        for c in range(len(chains)):
            acc = acc_ref[c]
            out = acc[:, :HEAD_DIM] / acc[:, HEAD_DIM:HEAD_DIM + 1]
            o_ref[0, pl.ds(kv[c] * ATT_GROUP, ATT_GROUP), q_rows[c], :] = (
                out.reshape(ATT_GROUP, ATT_ROWS, HEAD_DIM).astype(o_ref.dtype))
        return carry

    lax.fori_loop(0, ATT_KV_HEADS // ATT_GROUPS_PER_STEP, group_step, 0)


def _dsa_attn(q, k, v, qi, ki, wi, topk):
    b, _, tp, _ = q.shape
    tq = SEQ_ALIGN
    nq = tp // tq
    widths = sorted({-(-nq * c // 4) for c in range(1, 5)})
    widths = tuple((n, n * tq) for n in widths)
    n_chains = ATT_GROUPS_PER_STEP * (tq // ATT_ROWS)
    per_q = lambda n: pl.BlockSpec((1, n, tq, HEAD_DIM), lambda bi, i: (bi, 0, i, 0))
    per_b = lambda n, w: pl.BlockSpec((1, n, tp, w), lambda bi, i: (bi, 0, 0, 0),
                                      pipeline_mode=pl.Buffered(1))
    return pl.pallas_call(
        functools.partial(_dsa_attn_body, topk=topk, widths=widths),
        out_shape=jax.ShapeDtypeStruct((b, ATT_HEADS, tp, HEAD_DIM), _MXU),
        grid=(b, nq),
        in_specs=[per_q(IDX_HEADS),
                  pl.BlockSpec((1, tq, IDX_HEADS), lambda bi, i: (bi, i, 0)),
                  pl.BlockSpec((1, tp, IDX_DIM), lambda bi, i: (bi, 0, 0), pipeline_mode=pl.Buffered(1)),
                  per_q(ATT_HEADS), per_b(ATT_KV_HEADS, HEAD_DIM), per_b(ATT_KV_HEADS, LANES)],
        out_specs=per_q(ATT_HEADS),
        scratch_shapes=[pltpu.VMEM((tq, tp), jnp.int32),
                        pltpu.VMEM((tq, tp), _F32),
                        pltpu.VMEM((IDX_HEADS, tq, LANES), _F32),
                        pltpu.VMEM((n_chains, ATT_GROUP * ATT_ROWS, 1), _F32),
                        pltpu.VMEM((n_chains, ATT_GROUP * ATT_ROWS, LANES), _F32)],
        compiler_params=_params("parallel", "arbitrary"),
        name="dsa_attn",
    )(qi, wi, ki, q, k, v)


def _dsa_out_body(h_ref, y_ref, w_ref, o_ref):
    y = jnp.concatenate([y_ref[0, hd] for hd in range(ATT_HEADS)], axis=1)
    o_ref[0] = h_ref[0] + _dot(y, w_ref[...])


def _dsa_out(h, y, w_o):
    b, tp, d = h.shape
    tm = SEQ_ALIGN
    tile = pl.BlockSpec((1, tm, d), lambda bi, i: (bi, i, 0))
    return pl.pallas_call(
        _dsa_out_body,
        out_shape=jax.ShapeDtypeStruct((b, tp, d), _F32),
        grid=(b, tp // tm),
        in_specs=[tile, pl.BlockSpec((1, ATT_HEADS, tm, HEAD_DIM), lambda bi, i: (bi, 0, i, 0)),
                  _const_spec(w_o.shape)],
        out_specs=tile,
        compiler_params=_params("parallel", "parallel"),
        name="dsa_out",
    )(h, y, w_o.astype(_MXU))


def _dsa_layer(h, ng, w_in, q_g, k_g, w_o, topk):
    q, k, v, qi, ki, wi = _dsa_proj(h, ng, w_in, q_g, k_g)
    return _dsa_out(h, _dsa_attn(q, k, v, qi, ki, wi, topk), w_o)


def kernel(x, meta, norm_g, ffn_w_in, ffn_w_out, rk_mix, rk_w_r, rk_w_k, rk_w_v, rk_w_o, rk_w0, rk_w1,
           rk_w2, rk_a0, rk_a1, rk_a2, rk_g1, rk_g2, rk_k_k, rk_k_a, rk_r_k, rk_lnx_g, rk_lnx_b,
           at_w_in, at_q_g, at_k_g, at_w_o):
    b, seq, d = x.shape
    topk = min(TOPK_MAX, seq // 4)
    t = seq + N_META
    tp = -(-t // SEQ_ALIGN) * SEQ_ALIGN
    h = jnp.concatenate([jnp.broadcast_to(meta.astype(x.dtype)[None], (b, N_META, d)), x,
                         jnp.zeros((b, tp - t, d), x.dtype)], axis=1)
    rk = dict(mix=rk_mix, w_r=rk_w_r, w_k=rk_w_k, w_v=rk_w_v, w_o=rk_w_o, w0=rk_w0, w1=rk_w1, w2=rk_w2,
              a0=rk_a0, a1=rk_a1, a2=rk_a2, g1=rk_g1, g2=rk_g2, k_k=rk_k_k, k_a=rk_k_a, r_k=rk_r_k,
              lnx_g=rk_lnx_g, lnx_b=rk_lnx_b)
    n_mixers = 2
    for layer in range(norm_g.shape[0]):
        g = norm_g[layer]
        h = _ffn(h.reshape(b * tp, d), g[0], ffn_w_in[layer, 0], ffn_w_out[layer, 0]).reshape(b, tp, d)
        j = layer // n_mixers
        if layer % n_mixers == 0:
            h = _rwkv_layer(h, g[1], {name: val[j] for name, val in rk.items()})
        else:
            h = _dsa_layer(h, g[1], at_w_in[j], at_q_g[j], at_k_g[j], at_w_o[j], topk)
        h = _ffn(h.reshape(b * tp, d), g[2], ffn_w_in[layer, 1], ffn_w_out[layer, 1]).reshape(b, tp, d)
    return h[:, N_META:t]
```

```python
import functools

import jax
import jax.numpy as jnp
from jax import lax
from jax.experimental import pallas as pl
from jax.experimental.pallas import tpu as pltpu

D_MODEL = 1024
N_META = 16
D_FF = 2816
NORM_EPS = 1e-6
HEAD_DIM = 64
ROPE_DIM = HEAD_DIM // 4
ROPE_THETA = 500000.0
RWKV_HEADS = D_MODEL // HEAD_DIM
RWKV_LN_EPS = 64e-5
ATT_HEADS = D_MODEL // HEAD_DIM
ATT_KV_HEADS = 4
ATT_GROUP = ATT_HEADS // ATT_KV_HEADS
IDX_HEADS = 8
IDX_DIM = 64
TOPK_MAX = 256
ATT_Q_W = ATT_HEADS * HEAD_DIM
ATT_KV_W = ATT_KV_HEADS * HEAD_DIM
IDX_Q_W = IDX_HEADS * IDX_DIM

LANES = 128
SEQ_ALIGN = 256
VMEM_LIMIT = 56 * 1024 * 1024
FF_CHUNK = 256
WKV_CHUNK = 64
WKV_CHUNKS_PER_STEP = 4
WKV_PAIRS_PER_STEP = 2
ATT_ROWS = 128
ATT_SPAN = 1024
ATT_GROUPS_PER_STEP = 2
INT_MIN = -(2 ** 31)
NEG_BIG = -1e30
LOG2_E = 1.4426950408889634

_MXU = jnp.bfloat16
_F32 = jnp.float32


def _dot(a, b):
    return jnp.dot(a.astype(_MXU), b.astype(_MXU), preferred_element_type=_F32)


def _dot_nt(a, b):
    return lax.dot_general(a.astype(_MXU), b.astype(_MXU), (((1,), (1,)), ((), ())),
                           preferred_element_type=_F32)


def _dot_tn(a, b):
    return lax.dot_general(a.astype(_MXU), b.astype(_MXU), (((0,), (0,)), ((), ())),
                           preferred_element_type=_F32)


def _split2(x):
    hi = x.astype(_MXU)
    lo = (x - hi.astype(_F32)).astype(_MXU)
    return hi, lo


def _split3(x):
    hi = x.astype(_MXU)
    r1 = x - hi.astype(_F32)
    mid = r1.astype(_MXU)
    lo = (r1 - mid.astype(_F32)).astype(_MXU)
    return hi, mid, lo


def _rms(x, eps=NORM_EPS):
    return x * lax.rsqrt(jnp.mean(x * x, axis=-1, keepdims=True) + eps)


def _sigmoid(x):
    return 1.0 / (1.0 + jnp.exp(-x))


def _params(*sem):
    return pltpu.CompilerParams(dimension_semantics=sem, vmem_limit_bytes=VMEM_LIMIT)


def _const_spec(shape):
    n = len(shape)
    return pl.BlockSpec(shape, lambda *_: (0,) * n)


def _swiglu_residual(x, g_ref, win_ref, wout_ref):
    xb = (_rms(x) * g_ref[...]).astype(_MXU)
    acc = jnp.zeros(x.shape, _F32)
    for c in range(D_FF // FF_CHUNK):
        lo = c * FF_CHUNK
        gate = _dot(xb, win_ref[:, lo:lo + FF_CHUNK])
        up = _dot(xb, win_ref[:, D_FF + lo:D_FF + lo + FF_CHUNK])
        act = gate * _sigmoid(gate) * up
        acc = acc + _dot(act, wout_ref[lo:lo + FF_CHUNK, :])
    return x + 0.5 * acc


def _ffn_body(h_ref, g_ref, win_ref, wout_ref, o_ref):
    o_ref[...] = _swiglu_residual(h_ref[...], g_ref, win_ref, wout_ref)


def _ffn_tail_body(h_ref, next_ref, g_ref, win_ref, wout_ref, o_ref):
    x = jnp.concatenate([h_ref[0, N_META:, :], next_ref[0]], axis=0)
    o_ref[0] = _swiglu_residual(x, g_ref, win_ref, wout_ref)


def _ffn(h2, g, w_in, w_out):
    rows, d = h2.shape
    tm = 512 if rows % 512 == 0 else SEQ_ALIGN
    win, wout = w_in.astype(_MXU), w_out.astype(_MXU)
    return pl.pallas_call(
        _ffn_body,
        out_shape=jax.ShapeDtypeStruct((rows, d), _F32),
        grid=(rows // tm,),
        in_specs=[pl.BlockSpec((tm, d), lambda i: (i, 0)),
                  _const_spec((1, d)),
                  _const_spec(win.shape),
                  _const_spec(wout.shape)],
        out_specs=pl.BlockSpec((tm, d), lambda i: (i, 0)),
        compiler_params=_params("parallel"),
        name="ffn",
    )(h2, g.reshape(1, d), win, wout)


def _ffn_tail(h, g, w_in, w_out, seq):
    b, tp, d = h.shape
    tm = next((t for t in (512, SEQ_ALIGN) if seq % t == 0), None)
    if tm is None:
        return _ffn(h.reshape(b * tp, d), g, w_in, w_out).reshape(b, tp, d)[:, N_META:N_META + seq]
    win, wout = w_in.astype(_MXU), w_out.astype(_MXU)
    return pl.pallas_call(
        _ffn_tail_body,
        out_shape=jax.ShapeDtypeStruct((b, seq, d), _F32),
        grid=(b, seq // tm),
        in_specs=[pl.BlockSpec((1, tm, d), lambda bi, i: (bi, i, 0)),
                  pl.BlockSpec((1, N_META, d), lambda bi, i: (bi, (i + 1) * (tm // N_META), 0)),
                  _const_spec((1, d)),
                  _const_spec(win.shape),
                  _const_spec(wout.shape)],
        out_specs=pl.BlockSpec((1, tm, d), lambda bi, i: (bi, i, 0)),
        compiler_params=_params("parallel", "parallel"),
        name="ffn_tail",
    )(h, h, g.reshape(1, d), win, wout)


def _rwkv_prep_body(h_ref, hprev_ref, ng_ref, mix_ref, wr_ref, wk_ref, wv_ref, w1_ref, w2_ref,
                    a1_ref, a2_ref, g1_ref, g2_ref, w0_ref, a0_ref,
                    r_ref, k_ref, v_ref, lw_ref, a_ref, g_ref):
    ng = ng_ref[...]
    hn = _rms(h_ref[0]) * ng
    prev = (_rms(hprev_ref[0]) * ng)[7:8, :]
    prev = jnp.where(pl.program_id(1) == 0, 0.0, prev)
    row = lax.broadcasted_iota(jnp.int32, hn.shape, 0)
    shifted = jnp.where(row == 0, prev, pltpu.roll(hn, 1, 0))
    xx = shifted - hn

    def mixed(j):
        return (hn + xx * mix_ref[j:j + 1, :]).astype(_MXU)

    r_ref[0] = _dot(mixed(0), wr_ref[...])
    k_ref[0] = _dot(mixed(2), wk_ref[...])
    v_ref[0] = _dot(mixed(3), wv_ref[...])
    wl = w0_ref[...] + _dot(jnp.tanh(_dot(mixed(1), w1_ref[...])), w2_ref[...])
    lw_ref[0] = -0.6065306597126334 * _sigmoid(wl)
    a_ref[0] = _sigmoid(a0_ref[...] + _dot(_dot(mixed(4), a1_ref[...]), a2_ref[...]))
    g_ref[0] = _dot(_sigmoid(_dot(mixed(5), g1_ref[...])), g2_ref[...])


def _rwkv_prep(h, ng, mix, w_r, w_k, w_v, w1, w2, a1, a2, g1, g2, w0, a0):
    b, tp, d = h.shape
    tm = SEQ_ALIGN
    ws = [w.astype(_MXU) for w in (w_r, w_k, w_v, w1, w2, a1, a2, g1, g2)]
    act = pl.BlockSpec((1, tm, d), lambda bi, i: (bi, i, 0))
    prev = pl.BlockSpec((1, 8, d), lambda bi, i: (bi, jnp.maximum(i * (tm // 8) - 1, 0), 0))
    out = jax.ShapeDtypeStruct((b, tp, d), _F32)
    return pl.pallas_call(
        _rwkv_prep_body,
        out_shape=[out] * 6,
        grid=(b, tp // tm),
        in_specs=[act, prev, _const_spec((1, d)), _const_spec(mix.shape)]
                 + [_const_spec(w.shape) for w in ws]
                 + [_const_spec((1, d)), _const_spec((1, d))],
        out_specs=[act] * 6,
        compiler_params=_params("parallel", "parallel"),
        name="rwkv_prep",
    )(h, h, ng.reshape(1, d), mix, *ws, w0.reshape(1, d), a0.reshape(1, d))


def _wkv_body(r_ref, k_ref, v_ref, lw_ref, a_ref, kk_ref, ka_ref, rk_ref, lg_ref, lb_ref,
              o_ref, s_ref, *, n_chunks, n_pairs):
    c_len = WKV_CHUNK

    @pl.when(pl.program_id(2) == 0)
    def _():
        s_ref[...] = jnp.zeros(s_ref.shape, _F32)

    lane = lax.broadcasted_iota(jnp.int32, (c_len, LANES), 1)
    row = lax.broadcasted_iota(jnp.int32, (c_len, LANES), 0)
    first = lane < HEAD_DIM
    col = jnp.where(first, lane, lane - HEAD_DIM)
    strict = col < row
    incl = col <= row
    sq_r = lax.broadcasted_iota(jnp.int32, (LANES, LANES), 0)
    sq_c = lax.broadcasted_iota(jnp.int32, (LANES, LANES), 1)
    same_head = (sq_r < HEAD_DIM) == (sq_c < HEAD_DIM)
    eye = sq_r == sq_c
    tri_r = lax.broadcasted_iota(jnp.int32, (c_len, c_len), 0)
    tri_c = lax.broadcasted_iota(jnp.int32, (c_len, c_len), 1)
    tri = jnp.where(tri_c <= tri_r, 1.0, 0.0).astype(_MXU)

    def head_sum(x):
        s0 = jnp.sum(jnp.where(first, x, 0.0), axis=-1, keepdims=True)
        s1 = jnp.sum(jnp.where(first, 0.0, x), axis=-1, keepdims=True)
        return jnp.where(first, s0, s1)

    def blockdiag(x):
        f = first if x.shape[1] == LANES else jnp.concatenate([first] * (x.shape[1] // LANES), axis=1)
        xb = x.astype(_MXU)
        zero = jnp.zeros_like(xb)
        return jnp.concatenate([jnp.where(f, xb, zero), jnp.where(f, zero, xb)], axis=0)

    def apply(p, x):
        return jnp.dot(p.astype(_MXU), blockdiag(x), preferred_element_type=_F32)

    units = [(c, p) for c in range(n_chunks) for p in range(n_pairs)]
    st = {}
    for u in units:
        c, p = u
        sl = pl.ds(c * c_len, c_len)
        ln = slice(p * LANES, (p + 1) * LANES)
        r = r_ref[0, sl, ln]
        k_raw = k_ref[0, sl, ln]
        v = v_ref[0, sl, ln]
        lw = lw_ref[0, sl, ln]
        gate = a_ref[0, sl, ln]
        kk = k_raw * kk_ref[:, ln]
        kk = kk / jnp.maximum(jnp.sqrt(head_sum(kk * kk)), 1e-12)
        k = k_raw * (1.0 + (gate - 1.0) * ka_ref[:, ln])
        h3 = _split3(lw)
        cs = (jnp.dot(tri, h3[0], preferred_element_type=_F32)
              + jnp.dot(tri, h3[1], preferred_element_type=_F32)
              + jnp.dot(tri, h3[2], preferred_element_type=_F32))
        st[u] = dict(r=r, k=k, v=v, lw=lw, cs=cs, a_vec=-kk, b_vec=kk * gate)

    for u in units:
        d = st[u]
        cs, lw = d["cs"], d["lw"]
        cs_prev = cs - lw
        c_mid = cs[c_len // 2 - 1:c_len // 2, :]
        c_end = cs[c_len - 1:c_len, :]
        inv_rel = jnp.exp(c_mid - cs)
        to_end = jnp.exp(c_end - cs)
        d["a_abs"] = d["a_vec"] * jnp.exp(cs_prev)
        d["r_abs"] = d["r"] * jnp.exp(cs)
        d["b_end"] = d["b_vec"] * to_end
        d["k_end"] = d["k"] * to_end
        d["g_end"] = jnp.exp(c_end)
        lhs = jnp.concatenate([d["a_vec"] * jnp.exp(cs_prev - c_mid), d["r"] * jnp.exp(cs - c_mid)],
                              axis=0).astype(_MXU)
        d["sb"] = lax.dot_general(lhs, blockdiag(d["b_vec"] * inv_rel), (((1,), (1,)), ((), ())),
                                  preferred_element_type=_F32)
        d["sk"] = lax.dot_general(lhs, blockdiag(d["k"] * inv_rel), (((1,), (1,)), ((), ())),
                                  preferred_element_type=_F32)

    for u in units:
        d = st[u]
        d["p"] = jnp.where(strict, d["sb"][:c_len], 0.0)
        d["g_rb"] = jnp.where(incl, d["sb"][c_len:], 0.0)
        d["g_rk"] = jnp.where(incl, d["sk"][c_len:], 0.0)
        l_ak = jnp.where(strict, d["sk"][:c_len], 0.0)
        d["x"] = jnp.concatenate([d["a_abs"], apply(l_ak, d["v"])], axis=1)

    n_pow = c_len.bit_length() - 1
    for i in range(n_pow):
        for u in units:
            d = st[u]
            d["x"] = d["x"] + apply(d["p"], d["x"])
            if i + 1 < n_pow:
                d["p"] = apply(d["p"], d["p"])

    for u in units:
        d = st[u]
        x = d["x"]
        w_a = x[:, :LANES]
        u0 = x[:, LANES:]
        gx = apply(d["g_rb"], x)
        d["q_eff"] = d["r_abs"] + gx[:, :LANES]
        d["y0"] = gx[:, LANES:] + apply(d["g_rk"], d["v"])
        d["m"] = jnp.where(same_head, _dot_tn(w_a, d["b_end"]), 0.0) + jnp.where(eye, d["g_end"], 0.0)
        d["n"] = jnp.where(same_head, _dot_tn(u0, d["b_end"]) + _dot_tn(d["v"], d["k_end"]), 0.0)

    state = [s_ref[p] for p in range(n_pairs)]
    for u in units:
        c, p = u
        d = st[u]
        s_hi, s_lo = _split2(state[p])
        d["y"] = d["y0"] + _dot_nt(d["q_eff"], s_hi) + _dot_nt(d["q_eff"], s_lo)
        m_hi, m_lo = _split2(d["m"])
        state[p] = (jnp.dot(s_hi, m_hi, preferred_element_type=_F32)
                    + jnp.dot(s_hi, m_lo, preferred_element_type=_F32)
                    + jnp.dot(s_lo, m_hi, preferred_element_type=_F32) + d["n"])
    for p in range(n_pairs):
        s_ref[p] = state[p]

    for u in units:
        c, p = u
        d = st[u]
        ln = slice(p * LANES, (p + 1) * LANES)
        y = d["y"]
        mu = head_sum(y) * (1.0 / HEAD_DIM)
        yc = y - mu
        var = head_sum(yc * yc) * (1.0 / HEAD_DIM)
        yn = yc * lax.rsqrt(var + RWKV_LN_EPS) * lg_ref[:, ln] + lb_ref[:, ln]
        o_ref[0, pl.ds(c * c_len, c_len), ln] = yn + head_sum(d["r"] * d["k"] * rk_ref[:, ln]) * d["v"]


def _wkv(r, k, v, lw, a, k_k, k_a, r_k, ln_g, ln_b):
    b, tp, d = r.shape
    tc = WKV_CHUNK * WKV_CHUNKS_PER_STEP
    width = LANES * WKV_PAIRS_PER_STEP
    act = pl.BlockSpec((1, tc, width), lambda bi, p, s: (bi, s, p))
    par = pl.BlockSpec((1, width), lambda bi, p, s: (0, p))
    return pl.pallas_call(
        functools.partial(_wkv_body, n_chunks=WKV_CHUNKS_PER_STEP, n_pairs=WKV_PAIRS_PER_STEP),
        out_shape=jax.ShapeDtypeStruct((b, tp, d), _F32),
        grid=(b, d // width, tp // tc),
        in_specs=[act] * 5 + [par] * 5,
        out_specs=act,
        scratch_shapes=[pltpu.VMEM((WKV_PAIRS_PER_STEP, LANES, LANES), _F32)],
        compiler_params=_params("parallel", "parallel", "arbitrary"),
        name="wkv",
    )(r, k, v, lw, a, *(p.reshape(1, d) for p in (k_k, k_a, r_k, ln_g, ln_b)))


def _gated_out_body(h_ref, y_ref, g_ref, w_ref, o_ref):
    o_ref[...] = h_ref[...] + _dot(y_ref[...] * g_ref[...], w_ref[...])


def _gated_out(h2, y2, gate2, w):
    rows, d = h2.shape
    tm = 512 if rows % 512 == 0 else SEQ_ALIGN
    tile = pl.BlockSpec((tm, d), lambda i: (i, 0))
    return pl.pallas_call(
        _gated_out_body,
        out_shape=jax.ShapeDtypeStruct((rows, d), _F32),
        grid=(rows // tm,),
        in_specs=[tile] * 3 + [_const_spec(w.shape)],
        out_specs=tile,
        compiler_params=_params("parallel"),
        name="rwkv_out",
    )(h2, y2, gate2, w.astype(_MXU))


def _rwkv_layer(h, ng, p):
    b, tp, d = h.shape
    r, k, v, lw, a, g = _rwkv_prep(h, ng, p["mix"], p["w_r"], p["w_k"], p["w_v"], p["w1"], p["w2"],
                                   p["a1"], p["a2"], p["g1"], p["g2"], p["w0"], p["a0"])
    y = _wkv(r, k, v, lw, a, p["k_k"], p["k_a"], p["r_k"], p["lnx_g"], p["lnx_b"])
    flat = lambda t: t.reshape(b * tp, d)
    return _gated_out(flat(h), flat(y), flat(g), p["w_o"]).reshape(b, tp, d)


def _rope_tables(tp):
    half = ROPE_DIM // 2
    inv = ROPE_THETA ** (-jnp.arange(0, ROPE_DIM, 2, dtype=_F32) / ROPE_DIM)
    ang = jnp.arange(tp, dtype=_F32)[:, None] * inv[None, :]
    cos, sin = jnp.cos(ang), jnp.sin(ang)
    ones = jnp.ones((tp, HEAD_DIM - ROPE_DIM), _F32)
    zeros = jnp.zeros((tp, HEAD_DIM - ROPE_DIM), _F32)
    zh = jnp.zeros((tp, half), _F32)
    c = jnp.concatenate([cos, cos, ones], axis=1)
    s_up = jnp.concatenate([-sin, zh, zeros], axis=1)
    s_dn = jnp.concatenate([zh, sin, zeros], axis=1)
    return tuple(jnp.concatenate([t, t], axis=1) for t in (c, s_up, s_dn))


def _dsa_proj_body(h_ref, ng_ref, wq_ref, wk_ref, wv_ref, wqi_ref, ws_ref, qg_ref, kg_ref,
                   c_ref, su_ref, sd_ref, q_ref, k_ref, v_ref, qi_ref, ki_ref, wi_ref):
    hb = (_rms(h_ref[0]) * ng_ref[...]).astype(_MXU)
    c, s_up, s_dn = c_ref[...], su_ref[...], sd_ref[...]
    lane = lax.broadcasted_iota(jnp.int32, c.shape, 1)
    first = lane < HEAD_DIM

    def rope(x):
        return x * c + pltpu.roll(x, LANES - ROPE_DIM // 2, 1) * s_up + pltpu.roll(x, ROPE_DIM // 2, 1) * s_dn

    sq_r = lax.broadcasted_iota(jnp.int32, (LANES, LANES), 0)
    sq_c = lax.broadcasted_iota(jnp.int32, (LANES, LANES), 1)
    head_ones = jnp.where((sq_r < HEAD_DIM) == (sq_c < HEAD_DIM), 1.0, 0.0).astype(_MXU)

    def head_rms(x, gain):
        hi, lo = _split2(x * x)
        ss = (jnp.dot(hi, head_ones, preferred_element_type=_F32)
              + jnp.dot(lo, head_ones, preferred_element_type=_F32))
        return x * lax.rsqrt(ss * (1.0 / HEAD_DIM) + NORM_EPS) * gain

    def put_heads(dst_ref, pair, x):
        dst_ref[0, 2 * pair] = x[:, :HEAD_DIM].astype(dst_ref.dtype)
        dst_ref[0, 2 * pair + 1] = x[:, HEAD_DIM:].astype(dst_ref.dtype)

    def with_ones(x):
        return jnp.where(first, x, jnp.where(lane == HEAD_DIM, 1.0, 0.0)).astype(v_ref.dtype)

    q = _dot(hb, wq_ref[...])
    for p in range(ATT_Q_W // LANES):
        blk = rope(head_rms(q[:, p * LANES:(p + 1) * LANES], qg_ref[...]))
        put_heads(q_ref, p, blk * (HEAD_DIM ** -0.5 * LOG2_E))
    k = _dot(hb, wk_ref[...])
    v = _dot(hb, wv_ref[...])
    for p in range(ATT_KV_W // LANES):
        put_heads(k_ref, p, rope(head_rms(k[:, p * LANES:(p + 1) * LANES], kg_ref[...])))
        vp = v[:, p * LANES:(p + 1) * LANES]
        v_ref[0, 2 * p] = with_ones(vp)
        v_ref[0, 2 * p + 1] = with_ones(pltpu.roll(vp, HEAD_DIM, 1))
    qi = _dot(hb, wqi_ref[...])
    for p in range(IDX_Q_W // LANES):
        put_heads(qi_ref, p, rope(qi[:, p * LANES:(p + 1) * LANES]))
    small = _dot(hb, ws_ref[...])
    ki_ref[0] = rope(small)[:, :IDX_DIM].astype(ki_ref.dtype)
    wi_ref[0] = small[:, IDX_DIM:IDX_DIM + IDX_HEADS] * (IDX_HEADS ** -0.5 * IDX_DIM ** -0.5)


def _dsa_proj(h, ng, w_in, q_g, k_g):
    b, tp, d = h.shape
    tm = SEQ_ALIGN
    o0, o1, o2, o3, o4 = (0, ATT_Q_W, ATT_Q_W + ATT_KV_W, ATT_Q_W + 2 * ATT_KV_W,
                          ATT_Q_W + 2 * ATT_KV_W + IDX_Q_W)
    w = w_in.astype(_MXU)
    w_small = jnp.pad(w[:, o4:], ((0, 0), (0, LANES - (w.shape[1] - o4))))
    ws = [w[:, o0:o1], w[:, o1:o2], w[:, o2:o3], w[:, o3:o4], w_small]
    tables = _rope_tables(tp)
    pair = lambda gvec: jnp.concatenate([gvec, gvec]).reshape(1, LANES)
    heads = lambda n: (jax.ShapeDtypeStruct((b, n, tp, HEAD_DIM), _MXU),
                       pl.BlockSpec((1, n, tm, HEAD_DIM), lambda bi, i: (bi, 0, i, 0)))
    outs = [heads(ATT_HEADS), heads(ATT_KV_HEADS),
            (jax.ShapeDtypeStruct((b, ATT_KV_HEADS, tp, LANES), _MXU),
             pl.BlockSpec((1, ATT_KV_HEADS, tm, LANES), lambda bi, i: (bi, 0, i, 0))),
            heads(IDX_HEADS),
            (jax.ShapeDtypeStruct((b, tp, IDX_DIM), _MXU), pl.BlockSpec((1, tm, IDX_DIM), lambda bi, i: (bi, i, 0))),
            (jax.ShapeDtypeStruct((b, tp, IDX_HEADS), _F32), pl.BlockSpec((1, tm, IDX_HEADS), lambda bi, i: (bi, i, 0)))]
    tab = pl.BlockSpec((tm, LANES), lambda bi, i: (i, 0))
    return pl.pallas_call(
        _dsa_proj_body,
        out_shape=[o[0] for o in outs],
        grid=(b, tp // tm),
        in_specs=[pl.BlockSpec((1, tm, d), lambda bi, i: (bi, i, 0)), _const_spec((1, d))]
                 + [_const_spec(x.shape) for x in ws]
                 + [_const_spec((1, LANES)), _const_spec((1, LANES)), tab, tab, tab],
        out_specs=[o[1] for o in outs],
        compiler_params=_params("parallel", "parallel"),
        name="dsa_proj",
    )(h, ng.reshape(1, d), *ws, pair(q_g), pair(k_g), *tables)


def _dsa_attn_body(qi_ref, wi_ref, ki_ref, q_ref, k_ref, v_ref, o_ref,
                   keys_ref, bias_ref, wb_ref, m_ref, acc_ref, *, topk, widths):
    tq = SEQ_ALIGN
    tk = SEQ_ALIGN
    i = pl.program_id(1)
    n_tiles = i + 1
    n_key_tiles = keys_ref.shape[1] // tk

    keys_ref[...] = jnp.full(keys_ref.shape, INT_MIN, jnp.int32)
    wi = wi_ref[0]
    for hd in range(IDX_HEADS):
        wb_ref[hd] = jnp.broadcast_to(wi[:, hd:hd + 1], (tq, LANES))
    qi = qi_ref[0].reshape(IDX_HEADS * tq, IDX_DIM)
    row = lax.broadcasted_iota(jnp.int32, (tq, tk), 0)
    col = lax.broadcasted_iota(jnp.int32, (tq, tk), 1)

    def score_tiles(starts):
        zs = [_dot_nt(qi, ki_ref[0, pl.ds(start, tk), :]) for start in starts]
        for start, z in zip(starts, zs):
            score = jnp.zeros((tq, tk), _F32)
            for hd in range(IDX_HEADS):
                wb = wb_ref[hd]
                score = score + jnp.maximum(z[hd * tq:(hd + 1) * tq], 0.0) * jnp.concatenate([wb, wb], axis=1)
            bits = pltpu.bitcast(score, jnp.int32)
            key = bits ^ ((bits >> 31) & 0x7FFFFFFF)
            key = jnp.where(bits == INT_MIN, 0, key)
            key = jnp.maximum(key, INT_MIN + 1)
            causal = (start + col) <= (i * tq + row)
            keys_ref[:, pl.ds(start, tk)] = jnp.where(causal, key, INT_MIN)

    def score_pair(j, carry):
        start = pl.multiple_of(j * (2 * tk), 2 * tk)
        score_tiles([start, start + tk])
        return carry

    lax.fori_loop(0, n_tiles // 2, score_pair, 0)

    @pl.when(n_tiles % 2 == 1)
    def _():
        score_tiles([pl.multiple_of(i * tk, tk)])

    def select(width):
        kv = keys_ref[:, :width]

        def count(mask):
            return jnp.sum(mask.astype(jnp.int32), axis=-1, keepdims=True)

        def value_bit(it, thr):
            cand = thr + (jnp.int32(1) << (31 - it))
            return jnp.where(count(kv >= cand) >= topk, cand, thr)

        thr = lax.fori_loop(0, 32, value_bit, jnp.full((tq, 1), INT_MIN, jnp.int32))
        above = kv > thr
        equal = (kv == thr) & (thr > INT_MIN)
        need = topk - count(above)
        pos = lax.broadcasted_iota(jnp.int32, (tq, width), 1)

        def tie_limit():
            def pos_bit(it, last_short):
                cand = last_short + (jnp.int32(1) << (12 - it))
                return jnp.where(count(equal & (pos <= cand)) < need, cand, last_short)
            return lax.fori_loop(0, 13, pos_bit, jnp.full((tq, 1), -1, jnp.int32)) + 1

        has_ties = jnp.max(count(equal) - need) > 0
        limit = lax.cond(has_ties, tie_limit, lambda: jnp.full((tq, 1), width, jnp.int32))
        chosen = above | (equal & (pos <= limit))
        bias_ref[:, :width] = jnp.where(chosen, 0.0, NEG_BIG)

    lo = 0
    for n_q, width in widths:
        @pl.when((i >= lo) & (i < n_q))
        def _(width=width):
            select(width)
        lo = n_q

    rows = ATT_GROUP * ATT_ROWS
    tiles_per_span = ATT_SPAN // tk
    span_tiles = (n_key_tiles // tiles_per_span) * tiles_per_span
    n_spans = (jnp.minimum(n_tiles, span_tiles) + tiles_per_span - 1) // tiles_per_span
    chains = [(gl, qh) for gl in range(ATT_GROUPS_PER_STEP) for qh in range(tq // ATT_ROWS)]

    def group_step(gs, carry):
        kv = [gs * ATT_GROUPS_PER_STEP + gl for gl, _ in chains]
        q_rows = [slice(qh * ATT_ROWS, (qh + 1) * ATT_ROWS) for _, qh in chains]
        qs = [q_ref[0, pl.ds(kv[c] * ATT_GROUP, ATT_GROUP), q_rows[c], :].reshape(rows, HEAD_DIM)
              for c in range(len(chains))]
        m_ref[...] = jnp.full(m_ref.shape, NEG_BIG, _F32)
        acc_ref[...] = jnp.zeros(acc_ref.shape, _F32)

        def flash(start, width):
            s = [_dot_nt(qs[c], k_ref[0, kv[c], pl.ds(start, width), :]) for c in range(len(chains))]
            p = []
            for c in range(len(chains)):
                bias = bias_ref[q_rows[c], pl.ds(start, width)]
                sc = (s[c].reshape(ATT_GROUP, ATT_ROWS, width) + bias[None]).reshape(rows, width)
                m_prev = m_ref[c]
                m_new = jnp.maximum(m_prev, jnp.max(sc, axis=-1, keepdims=True))
                acc_ref[c] = jnp.exp2(m_prev - m_new) * acc_ref[c]
                m_ref[c] = m_new
                p.append(jnp.exp2(sc - m_new).astype(_MXU))
            for c in range(len(chains)):
                acc_ref[c] += jnp.dot(p[c], v_ref[0, kv[c], pl.ds(start, width), :],
                                      preferred_element_type=_F32)

        def span_step(j, carry):
            flash(pl.multiple_of(j * ATT_SPAN, ATT_SPAN), ATT_SPAN)
            return carry

        lax.fori_loop(0, n_spans, span_step, 0)
        if n_key_tiles > span_tiles:
            @pl.when(n_tiles > span_tiles)
            def _():
                flash(span_tiles * tk, (n_key_tiles - span_tiles) * tk)
        for c in range(len(chains)):
            acc = acc_ref[c]
            out = acc[:, :HEAD_DIM] / acc[:, HEAD_DIM:HEAD_DIM + 1]
            o_ref[0, pl.ds(kv[c] * ATT_GROUP, ATT_GROUP), q_rows[c], :] = (
                out.reshape(ATT_GROUP, ATT_ROWS, HEAD_DIM).astype(o_ref.dtype))
        return carry

    lax.fori_loop(0, ATT_KV_HEADS // ATT_GROUPS_PER_STEP, group_step, 0)


def _dsa_attn(q, k, v, qi, ki, wi, topk):
    b, _, tp, _ = q.shape
    tq = SEQ_ALIGN
    nq = tp // tq
    tiles_per_span = ATT_SPAN // tq
    widths = sorted(set(range(tiles_per_span, nq + 1, tiles_per_span)) | {nq})
    widths = tuple((n, n * tq) for n in widths)
    n_chains = ATT_GROUPS_PER_STEP * (tq // ATT_ROWS)
    per_q = lambda n: pl.BlockSpec((1, n, tq, HEAD_DIM), lambda bi, i: (bi, 0, i, 0))
    per_b = lambda n, w: pl.BlockSpec((1, n, tp, w), lambda bi, i: (bi, 0, 0, 0),
                                      pipeline_mode=pl.Buffered(1))
    return pl.pallas_call(
        functools.partial(_dsa_attn_body, topk=topk, widths=widths),
        out_shape=jax.ShapeDtypeStruct((b, ATT_HEADS, tp, HEAD_DIM), _MXU),
        grid=(b, nq),
        in_specs=[per_q(IDX_HEADS),
                  pl.BlockSpec((1, tq, IDX_HEADS), lambda bi, i: (bi, i, 0)),
                  pl.BlockSpec((1, tp, IDX_DIM), lambda bi, i: (bi, 0, 0), pipeline_mode=pl.Buffered(1)),
                  per_q(ATT_HEADS), per_b(ATT_KV_HEADS, HEAD_DIM), per_b(ATT_KV_HEADS, LANES)],
        out_specs=per_q(ATT_HEADS),
        scratch_shapes=[pltpu.VMEM((tq, tp), jnp.int32),
                        pltpu.VMEM((tq, tp), _F32),
                        pltpu.VMEM((IDX_HEADS, tq, LANES), _F32),
                        pltpu.VMEM((n_chains, ATT_GROUP * ATT_ROWS, 1), _F32),
                        pltpu.VMEM((n_chains, ATT_GROUP * ATT_ROWS, LANES), _F32)],
        compiler_params=_params("parallel", "arbitrary"),
        name="dsa_attn",
    )(qi, wi, ki, q, k, v)


def _dsa_out_body(h_ref, y_ref, w_ref, o_ref):
    y = jnp.concatenate([y_ref[0, hd] for hd in range(ATT_HEADS)], axis=1)
    o_ref[0] = h_ref[0] + _dot(y, w_ref[...])


def _dsa_out(h, y, w_o):
    b, tp, d = h.shape
    tm = SEQ_ALIGN
    tile = pl.BlockSpec((1, tm, d), lambda bi, i: (bi, i, 0))
    return pl.pallas_call(
        _dsa_out_body,
        out_shape=jax.ShapeDtypeStruct((b, tp, d), _F32),
        grid=(b, tp // tm),
        in_specs=[tile, pl.BlockSpec((1, ATT_HEADS, tm, HEAD_DIM), lambda bi, i: (bi, 0, i, 0)),
                  _const_spec(w_o.shape)],
        out_specs=tile,
        compiler_params=_params("parallel", "parallel"),
        name="dsa_out",
    )(h, y, w_o.astype(_MXU))


def _dsa_layer(h, ng, w_in, q_g, k_g, w_o, topk):
    q, k, v, qi, ki, wi = _dsa_proj(h, ng, w_in, q_g, k_g)
    return _dsa_out(h, _dsa_attn(q, k, v, qi, ki, wi, topk), w_o)


def kernel(x, meta, norm_g, ffn_w_in, ffn_w_out, rk_mix, rk_w_r, rk_w_k, rk_w_v, rk_w_o, rk_w0, rk_w1,
           rk_w2, rk_a0, rk_a1, rk_a2, rk_g1, rk_g2, rk_k_k, rk_k_a, rk_r_k, rk_lnx_g, rk_lnx_b,
           at_w_in, at_q_g, at_k_g, at_w_o):
    b, seq, d = x.shape
    topk = min(TOPK_MAX, seq // 4)
    t = seq + N_META
    tp = -(-t // SEQ_ALIGN) * SEQ_ALIGN
    h = jnp.concatenate([jnp.broadcast_to(meta.astype(x.dtype)[None], (b, N_META, d)), x,
                         jnp.zeros((b, tp - t, d), x.dtype)], axis=1)
    rk = dict(mix=rk_mix, w_r=rk_w_r, w_k=rk_w_k, w_v=rk_w_v, w_o=rk_w_o, w0=rk_w0, w1=rk_w1, w2=rk_w2,
              a0=rk_a0, a1=rk_a1, a2=rk_a2, g1=rk_g1, g2=rk_g2, k_k=rk_k_k, k_a=rk_k_a, r_k=rk_r_k,
              lnx_g=rk_lnx_g, lnx_b=rk_lnx_b)
    n_mixers = 2
    depth = norm_g.shape[0]
    for layer in range(depth):
        g = norm_g[layer]
        h = _ffn(h.reshape(b * tp, d), g[0], ffn_w_in[layer, 0], ffn_w_out[layer, 0]).reshape(b, tp, d)
        j = layer // n_mixers
        if layer % n_mixers == 0:
            h = _rwkv_layer(h, g[1], {name: val[j] for name, val in rk.items()})
        else:
            h = _dsa_layer(h, g[1], at_w_in[j], at_q_g[j], at_k_g[j], at_w_o[j], topk)
        if layer + 1 < depth:
            h = _ffn(h.reshape(b * tp, d), g[2], ffn_w_in[layer, 1], ffn_w_out[layer, 1]).reshape(b, tp, d)
    return _ffn_tail(h, norm_g[depth - 1, 2], ffn_w_in[depth - 1, 1], ffn_w_out[depth - 1, 1], seq)
```

```python
import functools

import jax
import jax.numpy as jnp
from jax import lax
from jax.experimental import pallas as pl
from jax.experimental.pallas import tpu as pltpu

D_MODEL = 1024
N_META = 16
D_FF = 2816
NORM_EPS = 1e-6
HEAD_DIM = 64
ROPE_DIM = HEAD_DIM // 4
ROPE_THETA = 500000.0
RWKV_HEADS = D_MODEL // HEAD_DIM
RWKV_LN_EPS = 64e-5
ATT_HEADS = D_MODEL // HEAD_DIM
ATT_KV_HEADS = 4
ATT_GROUP = ATT_HEADS // ATT_KV_HEADS
IDX_HEADS = 8
IDX_DIM = 64
TOPK_MAX = 256
ATT_Q_W = ATT_HEADS * HEAD_DIM
ATT_KV_W = ATT_KV_HEADS * HEAD_DIM
IDX_Q_W = IDX_HEADS * IDX_DIM

LANES = 128
SEQ_ALIGN = 256
VMEM_LIMIT = 56 * 1024 * 1024
FF_CHUNK = 256
WKV_CHUNK = 64
WKV_CHUNKS_PER_STEP = 4
WKV_INV_BASE = 4
WKV_PAIRS_PER_STEP = 8
ATT_ROWS = 128
ATT_SPAN = 1024
ATT_GROUPS_PER_STEP = 2
INT_MIN = -(2 ** 31)
NEG_BIG = -1e30
LOG2_E = 1.4426950408889634

_MXU = jnp.bfloat16
_F32 = jnp.float32


def _dot(a, b):
    return jnp.dot(a.astype(_MXU), b.astype(_MXU), preferred_element_type=_F32)


def _dot_nt(a, b):
    return lax.dot_general(a.astype(_MXU), b.astype(_MXU), (((1,), (1,)), ((), ())),
                           preferred_element_type=_F32)


def _dot_tn(a, b):
    return lax.dot_general(a.astype(_MXU), b.astype(_MXU), (((0,), (0,)), ((), ())),
                           preferred_element_type=_F32)


def _split2(x):
    hi = x.astype(_MXU)
    lo = (x - hi.astype(_F32)).astype(_MXU)
    return hi, lo


def _split3(x):
    hi = x.astype(_MXU)
    r1 = x - hi.astype(_F32)
    mid = r1.astype(_MXU)
    lo = (r1 - mid.astype(_F32)).astype(_MXU)
    return hi, mid, lo


def _rms(x, eps=NORM_EPS):
    return x * lax.rsqrt(jnp.mean(x * x, axis=-1, keepdims=True) + eps)


def _sigmoid(x):
    return 1.0 / (1.0 + jnp.exp(-x))


def _params(*sem):
    return pltpu.CompilerParams(dimension_semantics=sem, vmem_limit_bytes=VMEM_LIMIT)


def _const_spec(shape):
    n = len(shape)
    return pl.BlockSpec(shape, lambda *_: (0,) * n, pipeline_mode=pl.Buffered(1))


def _swiglu_residual(x, g_ref, win_ref, wout_ref):
    xb = (_rms(x) * g_ref[...]).astype(_MXU)
    acc = jnp.zeros(x.shape, _F32)
    for c in range(D_FF // FF_CHUNK):
        lo = c * FF_CHUNK
        gate = _dot(xb, win_ref[:, lo:lo + FF_CHUNK])
        up = _dot(xb, win_ref[:, D_FF + lo:D_FF + lo + FF_CHUNK])
        act = gate * _sigmoid(gate) * up
        acc = acc + _dot(act, wout_ref[lo:lo + FF_CHUNK, :])
    return x + 0.5 * acc


def _ffn_body(h_ref, g_ref, win_ref, wout_ref, o_ref):
    o_ref[...] = _swiglu_residual(h_ref[...], g_ref, win_ref, wout_ref)


def _ffn_gated_body(h_ref, y_ref, gate_ref, wo_ref, g_ref, win_ref, wout_ref, o_ref):
    x = h_ref[...] + _dot(y_ref[...] * gate_ref[...], wo_ref[...])
    o_ref[...] = _swiglu_residual(x, g_ref, win_ref, wout_ref)


def _shifted(ref, next_ref):
    return jnp.concatenate([ref[N_META:, :], next_ref[...]], axis=0)


def _ffn_tail_body(h_ref, hn_ref, g_ref, win_ref, wout_ref, o_ref):
    x = _shifted(h_ref.at[0], hn_ref.at[0])
    o_ref[0] = _swiglu_residual(x, g_ref, win_ref, wout_ref)


def _ffn_tail_heads_body(h_ref, hn_ref, y_ref, yn_ref, wo_ref, g_ref, win_ref, wout_ref, o_ref):
    y = jnp.concatenate([_shifted(y_ref.at[0, hd], yn_ref.at[0, hd]) for hd in range(ATT_HEADS)], axis=1)
    x = _shifted(h_ref.at[0], hn_ref.at[0]) + _dot(y, wo_ref[...])
    o_ref[0] = _swiglu_residual(x, g_ref, win_ref, wout_ref)


def _ffn(h2, g, w_in, w_out, gated=None):
    rows, d = h2.shape
    tm = 512 if rows % 512 == 0 else SEQ_ALIGN
    win, wout = w_in.astype(_MXU), w_out.astype(_MXU)
    tile = pl.BlockSpec((tm, d), lambda i: (i, 0))
    acts, specs = [h2], [tile]
    if gated is not None:
        y2, gate2, w_o = gated
        acts += [y2, gate2, w_o.astype(_MXU)]
        specs += [tile, tile, _const_spec(w_o.shape)]
    return pl.pallas_call(
        _ffn_body if gated is None else _ffn_gated_body,
        out_shape=jax.ShapeDtypeStruct((rows, d), _F32),
        grid=(rows // tm,),
        in_specs=specs + [_const_spec((1, d)), _const_spec(win.shape), _const_spec(wout.shape)],
        out_specs=tile,
        compiler_params=_params("parallel"),
        name="ffn" if gated is None else "ffn_gated",
    )(*acts, g.reshape(1, d), win, wout)


def _ffn_tail(h, g, w_in, w_out, seq, heads=None):
    b, tp, d = h.shape
    tm = next((t for t in (512, SEQ_ALIGN) if seq % t == 0), None)
    if tm is None:
        if heads is not None:
            h = _dsa_out(h, *heads)
        return _ffn(h.reshape(b * tp, d), g, w_in, w_out).reshape(b, tp, d)[:, N_META:N_META + seq]
    win, wout = w_in.astype(_MXU), w_out.astype(_MXU)
    nxt = lambda bi, i: (bi, (i + 1) * (tm // N_META), 0)
    acts = [h, h]
    specs = [pl.BlockSpec((1, tm, d), lambda bi, i: (bi, i, 0)), pl.BlockSpec((1, N_META, d), nxt)]
    if heads is not None:
        y, w_o = heads
        acts += [y, y, w_o.astype(_MXU)]
        specs += [pl.BlockSpec((1, ATT_HEADS, tm, HEAD_DIM), lambda bi, i: (bi, 0, i, 0)),
                  pl.BlockSpec((1, ATT_HEADS, N_META, HEAD_DIM), lambda bi, i: (bi, 0, (i + 1) * (tm // N_META), 0)),
                  _const_spec(w_o.shape)]
    return pl.pallas_call(
        _ffn_tail_body if heads is None else _ffn_tail_heads_body,
        out_shape=jax.ShapeDtypeStruct((b, seq, d), _F32),
        grid=(b, seq // tm),
        in_specs=specs + [_const_spec((1, d)), _const_spec(win.shape), _const_spec(wout.shape)],
        out_specs=pl.BlockSpec((1, tm, d), lambda bi, i: (bi, i, 0)),
        compiler_params=_params("parallel", "parallel"),
        name="ffn_tail",
    )(*acts, g.reshape(1, d), win, wout)


def _rwkv_prep_body(h_ref, hprev_ref, ng_ref, mix_ref, wr_ref, wk_ref, wv_ref, w1_ref, w2_ref,
                    a1_ref, a2_ref, g1_ref, g2_ref, w0_ref, a0_ref,
                    r_ref, k_ref, v_ref, lw_ref, a_ref, g_ref):
    ng = ng_ref[...]
    hn = _rms(h_ref[0]) * ng
    prev = (_rms(hprev_ref[0]) * ng)[7:8, :]
    prev = jnp.where(pl.program_id(1) == 0, 0.0, prev)
    row = lax.broadcasted_iota(jnp.int32, hn.shape, 0)
    shifted = jnp.where(row == 0, prev, pltpu.roll(hn, 1, 0))
    xx = shifted - hn

    def mixed(j):
        return (hn + xx * mix_ref[j:j + 1, :]).astype(_MXU)

    r_ref[0] = _dot(mixed(0), wr_ref[...])
    k_ref[0] = _dot(mixed(2), wk_ref[...])
    v_ref[0] = _dot(mixed(3), wv_ref[...])
    wl = w0_ref[...] + _dot(jnp.tanh(_dot(mixed(1), w1_ref[...])), w2_ref[...])
    lw_ref[0] = -0.6065306597126334 * _sigmoid(wl)
    a_ref[0] = _sigmoid(a0_ref[...] + _dot(_dot(mixed(4), a1_ref[...]), a2_ref[...]))
    g_ref[0] = _dot(_sigmoid(_dot(mixed(5), g1_ref[...])), g2_ref[...])


def _rwkv_prep(h, ng, mix, w_r, w_k, w_v, w1, w2, a1, a2, g1, g2, w0, a0):
    b, tp, d = h.shape
    tm = SEQ_ALIGN
    ws = [w.astype(_MXU) for w in (w_r, w_k, w_v, w1, w2, a1, a2, g1, g2)]
    act = pl.BlockSpec((1, tm, d), lambda bi, i: (bi, i, 0))
    prev = pl.BlockSpec((1, 8, d), lambda bi, i: (bi, jnp.maximum(i * (tm // 8) - 1, 0), 0))
    out = jax.ShapeDtypeStruct((b, tp, d), _F32)
    return pl.pallas_call(
        _rwkv_prep_body,
        out_shape=[out] * 6,
        grid=(b, tp // tm),
        in_specs=[act, prev, _const_spec((1, d)), _const_spec(mix.shape)]
                 + [_const_spec(w.shape) for w in ws]
                 + [_const_spec((1, d)), _const_spec((1, d))],
        out_specs=[act] * 6,
        compiler_params=_params("parallel", "parallel"),
        name="rwkv_prep",
    )(h, h, ng.reshape(1, d), mix, *ws, w0.reshape(1, d), a0.reshape(1, d))


def _wkv_body(r_ref, k_ref, v_ref, lw_ref, a_ref, kk_ref, ka_ref, rk_ref, lg_ref, lb_ref,
              o_ref, s_ref, *, n_chunks, n_pairs):
    c_len = WKV_CHUNK

    @pl.when(pl.program_id(2) == 0)
    def _():
        s_ref[...] = jnp.zeros(s_ref.shape, _F32)

    lane = lax.broadcasted_iota(jnp.int32, (c_len, LANES), 1)
    row = lax.broadcasted_iota(jnp.int32, (c_len, LANES), 0)
    first = lane < HEAD_DIM
    col = jnp.where(first, lane, lane - HEAD_DIM)
    strict = col < row
    incl = col <= row
    sq_r = lax.broadcasted_iota(jnp.int32, (LANES, LANES), 0)
    sq_c = lax.broadcasted_iota(jnp.int32, (LANES, LANES), 1)
    same_head = (sq_r < HEAD_DIM) == (sq_c < HEAD_DIM)
    eye = sq_r == sq_c
    tri_r = lax.broadcasted_iota(jnp.int32, (c_len, c_len), 0)
    tri_c = lax.broadcasted_iota(jnp.int32, (c_len, c_len), 1)
    tri = jnp.where(tri_c <= tri_r, 1.0, 0.0).astype(_MXU)

    def head_sum(x):
        s0 = jnp.sum(jnp.where(first, x, 0.0), axis=-1, keepdims=True)
        s1 = jnp.sum(jnp.where(first, 0.0, x), axis=-1, keepdims=True)
        return jnp.where(first, s0, s1)

    def blockdiag(x):
        f = first if x.shape[1] == LANES else jnp.concatenate([first] * (x.shape[1] // LANES), axis=1)
        xb = x.astype(_MXU)
        zero = jnp.zeros_like(xb)
        return jnp.concatenate([jnp.where(f, xb, zero), jnp.where(f, zero, xb)], axis=0)

    def apply(p, x):
        return jnp.dot(p.astype(_MXU), blockdiag(x), preferred_element_type=_F32)

    units = [(c, p) for c in range(n_chunks) for p in range(n_pairs)]
    st = {}
    for u in units:
        c, p = u
        sl = pl.ds(c * c_len, c_len)
        ln = slice(p * LANES, (p + 1) * LANES)
        r = r_ref[0, sl, ln]
        k_raw = k_ref[0, sl, ln]
        v = v_ref[0, sl, ln]
        lw = lw_ref[0, sl, ln]
        gate = a_ref[0, sl, ln]
        kk = k_raw * kk_ref[:, ln]
        kk = kk / jnp.maximum(jnp.sqrt(head_sum(kk * kk)), 1e-12)
        k = k_raw * (1.0 + (gate - 1.0) * ka_ref[:, ln])
        h3 = _split3(lw)
        cs = (jnp.dot(tri, h3[0], preferred_element_type=_F32)
              + jnp.dot(tri, h3[1], preferred_element_type=_F32)
              + jnp.dot(tri, h3[2], preferred_element_type=_F32))
        st[u] = dict(r=r, k=k, v=v, lw=lw, cs=cs, a_vec=-kk, b_vec=kk * gate)

    for u in units:
        d = st[u]
        cs, lw = d["cs"], d["lw"]
        cs_prev = cs - lw
        c_mid = cs[c_len // 2 - 1:c_len // 2, :]
        c_end = cs[c_len - 1:c_len, :]
        inv_rel = jnp.exp(c_mid - cs)
        to_end = jnp.exp(c_end - cs)
        d["a_abs"] = d["a_vec"] * jnp.exp(cs_prev)
        d["r_abs"] = d["r"] * jnp.exp(cs)
        d["b_end"] = d["b_vec"] * to_end
        d["k_end"] = d["k"] * to_end
        d["g_end"] = jnp.exp(c_end)
        lhs = jnp.concatenate([d["a_vec"] * jnp.exp(cs_prev - c_mid), d["r"] * jnp.exp(cs - c_mid)],
                              axis=0).astype(_MXU)
        d["sb"] = lax.dot_general(lhs, blockdiag(d["b_vec"] * inv_rel), (((1,), (1,)), ((), ())),
                                  preferred_element_type=_F32)
        d["sk"] = lax.dot_general(lhs, blockdiag(d["k"] * inv_rel), (((1,), (1,)), ((), ())),
                                  preferred_element_type=_F32)

    base_blk = (row // WKV_INV_BASE) == (col // WKV_INV_BASE)
    for u in units:
        d = st[u]
        d["l"] = jnp.where(strict, d["sb"][:c_len], 0.0)
        d["g_rb"] = jnp.where(incl, d["sb"][c_len:], 0.0)
        d["g_rk"] = jnp.where(incl, d["sk"][c_len:], 0.0)
        l_ak = jnp.where(strict, d["sk"][:c_len], 0.0)
        d["x0"] = jnp.concatenate([d["a_abs"], apply(l_ak, d["v"])], axis=1)
        l_base = jnp.where(base_blk, d["l"], 0.0)
        d["t"] = jnp.where(col == row, 1.0, l_base)
        d["q"] = apply(l_base, l_base)
    for u in units:
        d = st[u]
        d["t"] = d["t"] + apply(d["t"], d["q"])
    width = WKV_INV_BASE
    while width < c_len:
        off_blk = ((row // (2 * width)) == (col // (2 * width))) & ((row // width) != (col // width))
        for u in units:
            d = st[u]
            d["q"] = apply(jnp.where(off_blk, d["l"], 0.0), d["t"])
        for u in units:
            d = st[u]
            d["t"] = d["t"] + apply(d["t"], d["q"])
        width *= 2
    for u in units:
        d = st[u]
        d["x"] = apply(d["t"], d["x0"])

    for u in units:
        d = st[u]
        x = d["x"]
        w_a = x[:, :LANES]
        u0 = x[:, LANES:]
        gx = apply(d["g_rb"], x)
        d["q_eff"] = d["r_abs"] + gx[:, :LANES]
        d["y0"] = gx[:, LANES:] + apply(d["g_rk"], d["v"])
        d["m"] = jnp.where(same_head, _dot_tn(w_a, d["b_end"]), 0.0) + jnp.where(eye, d["g_end"], 0.0)
        d["n"] = jnp.where(same_head, _dot_tn(u0, d["b_end"]) + _dot_tn(d["v"], d["k_end"]), 0.0)

    state = [s_ref[p] for p in range(n_pairs)]
    for u in units:
        c, p = u
        d = st[u]
        s_hi, s_lo = _split2(state[p])
        d["y"] = d["y0"] + _dot_nt(d["q_eff"], s_hi) + _dot_nt(d["q_eff"], s_lo)
        m_hi, m_lo = _split2(d["m"])
        state[p] = (jnp.dot(s_hi, m_hi, preferred_element_type=_F32)
                    + jnp.dot(s_hi, m_lo, preferred_element_type=_F32)
                    + jnp.dot(s_lo, m_hi, preferred_element_type=_F32) + d["n"])
    for p in range(n_pairs):
        s_ref[p] = state[p]

    for u in units:
        c, p = u
        d = st[u]
        ln = slice(p * LANES, (p + 1) * LANES)
        y = d["y"]
        mu = head_sum(y) * (1.0 / HEAD_DIM)
        yc = y - mu
        var = head_sum(yc * yc) * (1.0 / HEAD_DIM)
        yn = yc * lax.rsqrt(var + RWKV_LN_EPS) * lg_ref[:, ln] + lb_ref[:, ln]
        o_ref[0, pl.ds(c * c_len, c_len), ln] = yn + head_sum(d["r"] * d["k"] * rk_ref[:, ln]) * d["v"]


def _wkv(r, k, v, lw, a, k_k, k_a, r_k, ln_g, ln_b):
    b, tp, d = r.shape
    tc = WKV_CHUNK * WKV_CHUNKS_PER_STEP
    width = LANES * WKV_PAIRS_PER_STEP
    act = pl.BlockSpec((1, tc, width), lambda bi, p, s: (bi, s, p))
    par = pl.BlockSpec((1, width), lambda bi, p, s: (0, p))
    return pl.pallas_call(
        functools.partial(_wkv_body, n_chunks=WKV_CHUNKS_PER_STEP, n_pairs=WKV_PAIRS_PER_STEP),
        out_shape=jax.ShapeDtypeStruct((b, tp, d), _F32),
        grid=(b, d // width, tp // tc),
        in_specs=[act] * 5 + [par] * 5,
        out_specs=act,
        scratch_shapes=[pltpu.VMEM((WKV_PAIRS_PER_STEP, LANES, LANES), _F32)],
        compiler_params=_params("parallel", "parallel", "arbitrary"),
        name="wkv",
    )(r, k, v, lw, a, *(p.reshape(1, d) for p in (k_k, k_a, r_k, ln_g, ln_b)))


def _gated_out_body(h_ref, y_ref, g_ref, w_ref, o_ref):
    o_ref[...] = h_ref[...] + _dot(y_ref[...] * g_ref[...], w_ref[...])


def _gated_out(h2, y2, gate2, w):
    rows, d = h2.shape
    tm = 512 if rows % 512 == 0 else SEQ_ALIGN
    tile = pl.BlockSpec((tm, d), lambda i: (i, 0))
    return pl.pallas_call(
        _gated_out_body,
        out_shape=jax.ShapeDtypeStruct((rows, d), _F32),
        grid=(rows // tm,),
        in_specs=[tile] * 3 + [_const_spec(w.shape)],
        out_specs=tile,
        compiler_params=_params("parallel"),
        name="rwkv_out",
    )(h2, y2, gate2, w.astype(_MXU))


def _rwkv_mix(h, ng, p):
    r, k, v, lw, a, g = _rwkv_prep(h, ng, p["mix"], p["w_r"], p["w_k"], p["w_v"], p["w1"], p["w2"],
                                   p["a1"], p["a2"], p["g1"], p["g2"], p["w0"], p["a0"])
    return _wkv(r, k, v, lw, a, p["k_k"], p["k_a"], p["r_k"], p["lnx_g"], p["lnx_b"]), g


def _rwkv_layer(h, ng, p):
    b, tp, d = h.shape
    y, g = _rwkv_mix(h, ng, p)
    flat = lambda t: t.reshape(b * tp, d)
    return _gated_out(flat(h), flat(y), flat(g), p["w_o"]).reshape(b, tp, d)


def _rope_tables(tp):
    half = ROPE_DIM // 2
    inv = ROPE_THETA ** (-jnp.arange(0, ROPE_DIM, 2, dtype=_F32) / ROPE_DIM)
    ang = jnp.arange(tp, dtype=_F32)[:, None] * inv[None, :]
    cos, sin = jnp.cos(ang), jnp.sin(ang)
    ones = jnp.ones((tp, HEAD_DIM - ROPE_DIM), _F32)
    zeros = jnp.zeros((tp, HEAD_DIM - ROPE_DIM), _F32)
    zh = jnp.zeros((tp, half), _F32)
    c = jnp.concatenate([cos, cos, ones], axis=1)
    s_up = jnp.concatenate([-sin, zh, zeros], axis=1)
    s_dn = jnp.concatenate([zh, sin, zeros], axis=1)
    return tuple(jnp.concatenate([t, t], axis=1) for t in (c, s_up, s_dn))


def _dsa_proj_body(h_ref, ng_ref, wq_ref, wk_ref, wv_ref, wqi_ref, ws_ref, qg_ref, kg_ref,
                   c_ref, su_ref, sd_ref, q_ref, k_ref, v_ref, qi_ref, ki_ref, wi_ref):
    hb = (_rms(h_ref[0]) * ng_ref[...]).astype(_MXU)
    c, s_up, s_dn = c_ref[...], su_ref[...], sd_ref[...]
    lane = lax.broadcasted_iota(jnp.int32, c.shape, 1)
    first = lane < HEAD_DIM

    def rope(x):
        return x * c + pltpu.roll(x, LANES - ROPE_DIM // 2, 1) * s_up + pltpu.roll(x, ROPE_DIM // 2, 1) * s_dn

    sq_r = lax.broadcasted_iota(jnp.int32, (LANES, LANES), 0)
    sq_c = lax.broadcasted_iota(jnp.int32, (LANES, LANES), 1)
    head_ones = jnp.where((sq_r < HEAD_DIM) == (sq_c < HEAD_DIM), 1.0, 0.0).astype(_MXU)

    def head_rms(x, gain):
        hi, lo = _split2(x * x)
        ss = (jnp.dot(hi, head_ones, preferred_element_type=_F32)
              + jnp.dot(lo, head_ones, preferred_element_type=_F32))
        return x * lax.rsqrt(ss * (1.0 / HEAD_DIM) + NORM_EPS) * gain

    def put_heads(dst_ref, pair, x):
        dst_ref[0, 2 * pair] = x[:, :HEAD_DIM].astype(dst_ref.dtype)
        dst_ref[0, 2 * pair + 1] = x[:, HEAD_DIM:].astype(dst_ref.dtype)

    def with_ones(x):
        return jnp.where(first, x, jnp.where(lane == HEAD_DIM, 1.0, 0.0)).astype(v_ref.dtype)

    q = _dot(hb, wq_ref[...])
    for p in range(ATT_Q_W // LANES):
        blk = rope(head_rms(q[:, p * LANES:(p + 1) * LANES], qg_ref[...]))
        put_heads(q_ref, p, blk * (HEAD_DIM ** -0.5 * LOG2_E))
    k = _dot(hb, wk_ref[...])
    v = _dot(hb, wv_ref[...])
    for p in range(ATT_KV_W // LANES):
        put_heads(k_ref, p, rope(head_rms(k[:, p * LANES:(p + 1) * LANES], kg_ref[...])))
        vp = v[:, p * LANES:(p + 1) * LANES]
        v_ref[0, 2 * p] = with_ones(vp)
        v_ref[0, 2 * p + 1] = with_ones(pltpu.roll(vp, HEAD_DIM, 1))
    qi = _dot(hb, wqi_ref[...])
    for p in range(IDX_Q_W // LANES):
        put_heads(qi_ref, p, rope(qi[:, p * LANES:(p + 1) * LANES]))
    small = _dot(hb, ws_ref[...])
    ki_ref[0] = rope(small)[:, :IDX_DIM].astype(ki_ref.dtype)
    wi_ref[0] = small[:, IDX_DIM:IDX_DIM + IDX_HEADS] * (IDX_HEADS ** -0.5 * IDX_DIM ** -0.5)


def _dsa_proj(h, ng, w_in, q_g, k_g):
    b, tp, d = h.shape
    tm = SEQ_ALIGN
    o0, o1, o2, o3, o4 = (0, ATT_Q_W, ATT_Q_W + ATT_KV_W, ATT_Q_W + 2 * ATT_KV_W,
                          ATT_Q_W + 2 * ATT_KV_W + IDX_Q_W)
    w = w_in.astype(_MXU)
    w_small = jnp.pad(w[:, o4:], ((0, 0), (0, LANES - (w.shape[1] - o4))))
    ws = [w[:, o0:o1], w[:, o1:o2], w[:, o2:o3], w[:, o3:o4], w_small]
    tables = _rope_tables(tp)
    pair = lambda gvec: jnp.concatenate([gvec, gvec]).reshape(1, LANES)
    heads = lambda n: (jax.ShapeDtypeStruct((b, n, tp, HEAD_DIM), _MXU),
                       pl.BlockSpec((1, n, tm, HEAD_DIM), lambda bi, i: (bi, 0, i, 0)))
    outs = [heads(ATT_HEADS), heads(ATT_KV_HEADS),
            (jax.ShapeDtypeStruct((b, ATT_KV_HEADS, tp, LANES), _MXU),
             pl.BlockSpec((1, ATT_KV_HEADS, tm, LANES), lambda bi, i: (bi, 0, i, 0))),
            heads(IDX_HEADS),
            (jax.ShapeDtypeStruct((b, tp, IDX_DIM), _MXU), pl.BlockSpec((1, tm, IDX_DIM), lambda bi, i: (bi, i, 0))),
            (jax.ShapeDtypeStruct((b, tp, IDX_HEADS), _F32), pl.BlockSpec((1, tm, IDX_HEADS), lambda bi, i: (bi, i, 0)))]
    tab = pl.BlockSpec((tm, LANES), lambda bi, i: (i, 0))
    return pl.pallas_call(
        _dsa_proj_body,
        out_shape=[o[0] for o in outs],
        grid=(b, tp // tm),
        in_specs=[pl.BlockSpec((1, tm, d), lambda bi, i: (bi, i, 0)), _const_spec((1, d))]
                 + [_const_spec(x.shape) for x in ws]
                 + [_const_spec((1, LANES)), _const_spec((1, LANES)), tab, tab, tab],
        out_specs=[o[1] for o in outs],
        compiler_params=_params("parallel", "parallel"),
        name="dsa_proj",
    )(h, ng.reshape(1, d), *ws, pair(q_g), pair(k_g), *tables)


def _dsa_attn_body(qi_ref, wi_ref, ki_ref, q_ref, k_ref, v_ref, o_ref,
                   keys_ref, bias_ref, wb_ref, m_ref, acc_ref, hi_ref, lo_ref, eq_ref, *, topk, widths):
    tq = SEQ_ALIGN
    tk = SEQ_ALIGN
    i = pl.program_id(1)
    n_tiles = i + 1
    n_key_tiles = keys_ref.shape[1] // tk

    keys_ref[...] = jnp.full(keys_ref.shape, INT_MIN, jnp.int32)
    wi = wi_ref[0]
    for hd in range(IDX_HEADS):
        wb_ref[hd] = jnp.broadcast_to(wi[:, hd:hd + 1], (tq, LANES))
    qi = qi_ref[0].reshape(IDX_HEADS * tq, IDX_DIM)
    row = lax.broadcasted_iota(jnp.int32, (tq, tk), 0)
    col = lax.broadcasted_iota(jnp.int32, (tq, tk), 1)

    def score_tiles(starts):
        zs = [_dot_nt(qi, ki_ref[0, pl.ds(start, tk), :]) for start in starts]
        for start, z in zip(starts, zs):
            score = jnp.zeros((tq, tk), _F32)
            for hd in range(IDX_HEADS):
                wb = wb_ref[hd]
                score = score + jnp.maximum(z[hd * tq:(hd + 1) * tq], 0.0) * jnp.concatenate([wb, wb], axis=1)
            bits = pltpu.bitcast(score, jnp.int32)
            key = bits ^ ((bits >> 31) & 0x7FFFFFFF)
            key = jnp.where(bits == INT_MIN, 0, key)
            key = jnp.maximum(key, INT_MIN + 1)
            causal = (start + col) <= (i * tq + row)
            keys_ref[:, pl.ds(start, tk)] = jnp.where(causal, key, INT_MIN)

    def score_pair(j, carry):
        start = pl.multiple_of(j * (2 * tk), 2 * tk)
        score_tiles([start, start + tk])
        return carry

    lax.fori_loop(0, n_tiles // 2, score_pair, 0)

    @pl.when(n_tiles % 2 == 1)
    def _():
        score_tiles([pl.multiple_of(i * tk, tk)])

    def select(width):
        kv = keys_ref[:, :width]

        def count(mask):
            return jnp.sum(mask.astype(jnp.int32), axis=-1, keepdims=True)

        i16 = jnp.int16
        n_blk = width // LANES
        blocks = [slice(j * LANES, (j + 1) * LANES) for j in range(n_blk)]
        hi_ref[:, :width] = (kv >> 16).astype(i16)
        lo_ref[:, :width] = ((kv & 0xFFFF) - 32768).astype(i16)
        zero = jnp.zeros((tq, LANES), i16)
        one = jnp.ones((tq, LANES), i16)

        def lanes16(x):
            return jnp.broadcast_to(x, (tq, LANES)).astype(i16)

        def total(acc):
            return jnp.sum(acc.astype(jnp.int32), axis=-1, keepdims=True)

        def hi_bit(it, t):
            cand = t + (jnp.int32(1) << (15 - it))
            cb = lanes16(cand)
            acc = zero
            for sl in blocks:
                acc = acc + jnp.where(hi_ref[:, sl] >= cb, one, zero)
            return jnp.where(total(acc) >= topk, cand, t)

        t_hi = lax.fori_loop(0, 16, hi_bit, jnp.full((tq, 1), -32768, jnp.int32))
        tb = lanes16(t_hi)
        acc = zero
        for sl in blocks:
            hi = hi_ref[:, sl]
            acc = acc + jnp.where(hi > tb, one, zero)
            eq_ref[:, sl] = jnp.where(hi == tb, one, zero)
        n_above_hi = total(acc)

        def lo_bit(it, t):
            cand = t + (jnp.int32(1) << (15 - it))
            cb = lanes16(cand)
            acc = zero
            for sl in blocks:
                acc = acc + jnp.where(lo_ref[:, sl] >= cb, eq_ref[:, sl], zero)
            return jnp.where(n_above_hi + total(acc) >= topk, cand, t)

        t_lo = lax.fori_loop(0, 16, lo_bit, jnp.full((tq, 1), -32768, jnp.int32))
        thr = (t_hi << 16) | (t_lo + 32768)
        above = kv > thr
        equal = (kv == thr) & (thr > INT_MIN)
        need = topk - count(above)
        pos = lax.broadcasted_iota(jnp.int32, (tq, width), 1)

        def tie_limit():
            def pos_bit(it, last_short):
                cand = last_short + (jnp.int32(1) << (12 - it))
                return jnp.where(count(equal & (pos <= cand)) < need, cand, last_short)
            return lax.fori_loop(0, 13, pos_bit, jnp.full((tq, 1), -1, jnp.int32)) + 1

        has_ties = jnp.max(count(equal) - need) > 0
        limit = lax.cond(has_ties, tie_limit, lambda: jnp.full((tq, 1), width, jnp.int32))
        chosen = above | (equal & (pos <= limit))
        bias_ref[:, :width] = jnp.where(chosen, 0.0, NEG_BIG)

    lo = 0
    for n_q, width in widths:
        @pl.when((i >= lo) & (i < n_q))
        def _(width=width):
            select(width)
        lo = n_q

    rows = ATT_GROUP * ATT_ROWS
    tiles_per_span = ATT_SPAN // tk
    span_tiles = (n_key_tiles // tiles_per_span) * tiles_per_span
    n_spans = (jnp.minimum(n_tiles, span_tiles) + tiles_per_span - 1) // tiles_per_span
    chains = [(gl, qh) for gl in range(ATT_GROUPS_PER_STEP) for qh in range(tq // ATT_ROWS)]

    def group_step(gs, carry):
        kv = [gs * ATT_GROUPS_PER_STEP + gl for gl, _ in chains]
        q_rows = [slice(qh * ATT_ROWS, (qh + 1) * ATT_ROWS) for _, qh in chains]
        qs = [q_ref[0, pl.ds(kv[c] * ATT_GROUP, ATT_GROUP), q_rows[c], :].reshape(rows, HEAD_DIM)
              for c in range(len(chains))]
        m_ref[...] = jnp.full(m_ref.shape, NEG_BIG, _F32)
        acc_ref[...] = jnp.zeros(acc_ref.shape, _F32)

        def flash(start, width):
            s = [_dot_nt(qs[c], k_ref[0, kv[c], pl.ds(start, width), :]) for c in range(len(chains))]
            p = []
            for c in range(len(chains)):
                bias = bias_ref[q_rows[c], pl.ds(start, width)]
                sc = (s[c].reshape(ATT_GROUP, ATT_ROWS, width) + bias[None]).reshape(rows, width)
                m_prev = m_ref[c]
                m_new = jnp.maximum(m_prev, jnp.max(sc, axis=-1, keepdims=True))
                acc_ref[c] = jnp.exp2(m_prev - m_new) * acc_ref[c]
                m_ref[c] = m_new
                p.append(jnp.exp2(sc - m_new).astype(_MXU))
            for c in range(len(chains)):
                acc_ref[c] += jnp.dot(p[c], v_ref[0, kv[c], pl.ds(start, width), :],
                                      preferred_element_type=_F32)

        def span_step(j, carry):
            flash(pl.multiple_of(j * ATT_SPAN, ATT_SPAN), ATT_SPAN)
            return carry

        lax.fori_loop(0, n_spans, span_step, 0)
        if n_key_tiles > span_tiles:
            @pl.when(n_tiles > span_tiles)
            def _():
                flash(span_tiles * tk, (n_key_tiles - span_tiles) * tk)
        for c in range(len(chains)):
            acc = acc_ref[c]
            out = acc[:, :HEAD_DIM] / acc[:, HEAD_DIM:HEAD_DIM + 1]
            o_ref[0, pl.ds(kv[c] * ATT_GROUP, ATT_GROUP), q_rows[c], :] = (
                out.reshape(ATT_GROUP, ATT_ROWS, HEAD_DIM).astype(o_ref.dtype))
        return carry

    lax.fori_loop(0, ATT_KV_HEADS // ATT_GROUPS_PER_STEP, group_step, 0)


def _dsa_attn(q, k, v, qi, ki, wi, topk):
    b, _, tp, _ = q.shape
    tq = SEQ_ALIGN
    nq = tp // tq
    tiles_per_span = ATT_SPAN // tq
    widths = sorted(set(range(tiles_per_span, nq - tiles_per_span + 1, tiles_per_span)) | {nq})
    widths = tuple((n, n * tq) for n in widths)
    n_chains = ATT_GROUPS_PER_STEP * (tq // ATT_ROWS)
    per_q = lambda n: pl.BlockSpec((1, n, tq, HEAD_DIM), lambda bi, i: (bi, 0, i, 0))
    per_b = lambda n, w: pl.BlockSpec((1, n, tp, w), lambda bi, i: (bi, 0, 0, 0),
                                      pipeline_mode=pl.Buffered(1))
    return pl.pallas_call(
        functools.partial(_dsa_attn_body, topk=topk, widths=widths),
        out_shape=jax.ShapeDtypeStruct((b, ATT_HEADS, tp, HEAD_DIM), _MXU),
        grid=(b, nq),
        in_specs=[per_q(IDX_HEADS),
                  pl.BlockSpec((1, tq, IDX_HEADS), lambda bi, i: (bi, i, 0)),
                  pl.BlockSpec((1, tp, IDX_DIM), lambda bi, i: (bi, 0, 0), pipeline_mode=pl.Buffered(1)),
                  per_q(ATT_HEADS), per_b(ATT_KV_HEADS, HEAD_DIM), per_b(ATT_KV_HEADS, LANES)],
        out_specs=per_q(ATT_HEADS),
        scratch_shapes=[pltpu.VMEM((tq, tp), jnp.int32),
                        pltpu.VMEM((tq, tp), _F32),
                        pltpu.VMEM((IDX_HEADS, tq, LANES), _F32),
                        pltpu.VMEM((n_chains, ATT_GROUP * ATT_ROWS, 1), _F32),
                        pltpu.VMEM((n_chains, ATT_GROUP * ATT_ROWS, LANES), _F32)]
                       + [pltpu.VMEM((tq, tp), jnp.int16)] * 3,
        compiler_params=_params("parallel", "arbitrary"),
        name="dsa_attn",
    )(qi, wi, ki, q, k, v)


def _dsa_out_body(h_ref, y_ref, w_ref, o_ref):
    y = jnp.concatenate([y_ref[0, hd] for hd in range(ATT_HEADS)], axis=1)
    o_ref[0] = h_ref[0] + _dot(y, w_ref[...])


def _dsa_out(h, y, w_o):
    b, tp, d = h.shape
    tm = SEQ_ALIGN
    tile = pl.BlockSpec((1, tm, d), lambda bi, i: (bi, i, 0))
    return pl.pallas_call(
        _dsa_out_body,
        out_shape=jax.ShapeDtypeStruct((b, tp, d), _F32),
        grid=(b, tp // tm),
        in_specs=[tile, pl.BlockSpec((1, ATT_HEADS, tm, HEAD_DIM), lambda bi, i: (bi, 0, i, 0)),
                  _const_spec(w_o.shape)],
        out_specs=tile,
        compiler_params=_params("parallel", "parallel"),
        name="dsa_out",
    )(h, y, w_o.astype(_MXU))


def _dsa_mix(h, ng, w_in, q_g, k_g, topk):
    q, k, v, qi, ki, wi = _dsa_proj(h, ng, w_in, q_g, k_g)
    return _dsa_attn(q, k, v, qi, ki, wi, topk)


def _dsa_layer(h, ng, w_in, q_g, k_g, w_o, topk):
    return _dsa_out(h, _dsa_mix(h, ng, w_in, q_g, k_g, topk), w_o)


def kernel(x, meta, norm_g, ffn_w_in, ffn_w_out, rk_mix, rk_w_r, rk_w_k, rk_w_v, rk_w_o, rk_w0, rk_w1,
           rk_w2, rk_a0, rk_a1, rk_a2, rk_g1, rk_g2, rk_k_k, rk_k_a, rk_r_k, rk_lnx_g, rk_lnx_b,
           at_w_in, at_q_g, at_k_g, at_w_o):
    b, seq, d = x.shape
    topk = min(TOPK_MAX, seq // 4)
    t = seq + N_META
    tp = -(-t // SEQ_ALIGN) * SEQ_ALIGN
    h = jnp.concatenate([jnp.broadcast_to(meta.astype(x.dtype)[None], (b, N_META, d)), x,
                         jnp.zeros((b, tp - t, d), x.dtype)], axis=1)
    rk = dict(mix=rk_mix, w_r=rk_w_r, w_k=rk_w_k, w_v=rk_w_v, w_o=rk_w_o, w0=rk_w0, w1=rk_w1, w2=rk_w2,
              a0=rk_a0, a1=rk_a1, a2=rk_a2, g1=rk_g1, g2=rk_g2, k_k=rk_k_k, k_a=rk_k_a, r_k=rk_r_k,
              lnx_g=rk_lnx_g, lnx_b=rk_lnx_b)
    n_mixers = 2
    depth = norm_g.shape[0]
    flat = lambda a: a.reshape(b * tp, d)
    for layer in range(depth):
        g = norm_g[layer]
        last = layer + 1 == depth
        w_in2, w_out2 = ffn_w_in[layer, 1], ffn_w_out[layer, 1]
        h = _ffn(flat(h), g[0], ffn_w_in[layer, 0], ffn_w_out[layer, 0]).reshape(b, tp, d)
        j = layer // n_mixers
        if layer % n_mixers == 0:
            p = {name: val[j] for name, val in rk.items()}
            if last:
                out = _ffn_tail(_rwkv_layer(h, g[1], p), g[2], w_in2, w_out2, seq)
            else:
                y, gate = _rwkv_mix(h, g[1], p)
                h = _ffn(flat(h), g[2], w_in2, w_out2, gated=(flat(y), flat(gate), p["w_o"])).reshape(b, tp, d)
        else:
            if last:
                y = _dsa_mix(h, g[1], at_w_in[j], at_q_g[j], at_k_g[j], topk)
                out = _ffn_tail(h, g[2], w_in2, w_out2, seq, heads=(y, at_w_o[j]))
            else:
                h = _dsa_layer(h, g[1], at_w_in[j], at_q_g[j], at_k_g[j], at_w_o[j], topk)
                h = _ffn(flat(h), g[2], w_in2, w_out2).reshape(b, tp, d)
    return out
```

```python
import functools

import jax
import jax.numpy as jnp
from jax import lax
from jax.experimental import pallas as pl
from jax.experimental.pallas import tpu as pltpu

D_MODEL = 1024
N_META = 16
D_FF = 2816
NORM_EPS = 1e-6
HEAD_DIM = 64
ROPE_DIM = HEAD_DIM // 4
ROPE_THETA = 500000.0
RWKV_HEADS = D_MODEL // HEAD_DIM
RWKV_LN_EPS = 64e-5
ATT_HEADS = D_MODEL // HEAD_DIM
ATT_KV_HEADS = 4
ATT_GROUP = ATT_HEADS // ATT_KV_HEADS
IDX_HEADS = 8
IDX_DIM = 64
TOPK_MAX = 256
ATT_Q_W = ATT_HEADS * HEAD_DIM
ATT_KV_W = ATT_KV_HEADS * HEAD_DIM
IDX_Q_W = IDX_HEADS * IDX_DIM

LANES = 128
SEQ_ALIGN = 256
VMEM_LIMIT = 56 * 1024 * 1024
FF_CHUNK = 256
WKV_CHUNK = 64
WKV_CHUNKS_PER_STEP = 4
WKV_INV_BASE = 4
WKV_PAIRS_PER_STEP = 8
ATT_ROWS = 128
ATT_SPAN = 1024
ATT_GROUPS_PER_STEP = 2
SELECT_CLASS_TILES = 2
INT_MIN = -(2 ** 31)
NEG_BIG = -1e30
LOG2_E = 1.4426950408889634

_MXU = jnp.bfloat16
_F32 = jnp.float32


def _dot(a, b):
    return jnp.dot(a.astype(_MXU), b.astype(_MXU), preferred_element_type=_F32)


def _dot_nt(a, b):
    return lax.dot_general(a.astype(_MXU), b.astype(_MXU), (((1,), (1,)), ((), ())),
                           preferred_element_type=_F32)


def _dot_tn(a, b):
    return lax.dot_general(a.astype(_MXU), b.astype(_MXU), (((0,), (0,)), ((), ())),
                           preferred_element_type=_F32)


def _split2(x):
    hi = x.astype(_MXU)
    lo = (x - hi.astype(_F32)).astype(_MXU)
    return hi, lo


def _split3(x):
    hi = x.astype(_MXU)
    r1 = x - hi.astype(_F32)
    mid = r1.astype(_MXU)
    lo = (r1 - mid.astype(_F32)).astype(_MXU)
    return hi, mid, lo


def _rms(x, eps=NORM_EPS):
    return x * lax.rsqrt(jnp.mean(x * x, axis=-1, keepdims=True) + eps)


def _sigmoid(x):
    return 1.0 / (1.0 + jnp.exp(-x))


def _params(*sem):
    return pltpu.CompilerParams(dimension_semantics=sem, vmem_limit_bytes=VMEM_LIMIT)


def _const_spec(shape):
    n = len(shape)
    return pl.BlockSpec(shape, lambda *_: (0,) * n, pipeline_mode=pl.Buffered(1))


def _swiglu_residual(x, g_ref, win_ref, wout_ref):
    xb = (_rms(x) * g_ref[...]).astype(_MXU)
    acc = jnp.zeros(x.shape, _F32)
    for c in range(D_FF // FF_CHUNK):
        lo = c * FF_CHUNK
        gate = _dot(xb, win_ref[:, lo:lo + FF_CHUNK])
        up = _dot(xb, win_ref[:, D_FF + lo:D_FF + lo + FF_CHUNK])
        act = gate * _sigmoid(gate) * up
        acc = acc + _dot(act, wout_ref[lo:lo + FF_CHUNK, :])
    return x + 0.5 * acc


def _ffn_body(h_ref, g_ref, win_ref, wout_ref, o_ref):
    o_ref[...] = _swiglu_residual(h_ref[...], g_ref, win_ref, wout_ref)


def _ffn_gated_body(h_ref, y_ref, gate_ref, wo_ref, g_ref, win_ref, wout_ref, o_ref):
    x = h_ref[...] + _dot(y_ref[...] * gate_ref[...], wo_ref[...])
    o_ref[...] = _swiglu_residual(x, g_ref, win_ref, wout_ref)


def _shifted(ref, next_ref):
    return jnp.concatenate([ref[N_META:, :], next_ref[...]], axis=0)


def _ffn_tail_body(h_ref, hn_ref, g_ref, win_ref, wout_ref, o_ref):
    x = _shifted(h_ref.at[0], hn_ref.at[0])
    o_ref[0] = _swiglu_residual(x, g_ref, win_ref, wout_ref)


def _ffn_tail_heads_body(h_ref, hn_ref, y_ref, yn_ref, wo_ref, g_ref, win_ref, wout_ref, o_ref):
    y = jnp.concatenate([_shifted(y_ref.at[0, hd], yn_ref.at[0, hd]) for hd in range(ATT_HEADS)], axis=1)
    x = _shifted(h_ref.at[0], hn_ref.at[0]) + _dot(y, wo_ref[...])
    o_ref[0] = _swiglu_residual(x, g_ref, win_ref, wout_ref)


def _ffn(h2, g, w_in, w_out, gated=None):
    rows, d = h2.shape
    tm = 512 if rows % 512 == 0 else SEQ_ALIGN
    win, wout = w_in.astype(_MXU), w_out.astype(_MXU)
    tile = pl.BlockSpec((tm, d), lambda i: (i, 0))
    acts, specs = [h2], [tile]
    if gated is not None:
        y2, gate2, w_o = gated
        acts += [y2, gate2, w_o.astype(_MXU)]
        specs += [tile, tile, _const_spec(w_o.shape)]
    return pl.pallas_call(
        _ffn_body if gated is None else _ffn_gated_body,
        out_shape=jax.ShapeDtypeStruct((rows, d), _F32),
        grid=(rows // tm,),
        in_specs=specs + [_const_spec((1, d)), _const_spec(win.shape), _const_spec(wout.shape)],
        out_specs=tile,
        compiler_params=_params("parallel"),
        name="ffn" if gated is None else "ffn_gated",
    )(*acts, g.reshape(1, d), win, wout)


def _ffn_tail(h, g, w_in, w_out, seq, heads=None):
    b, tp, d = h.shape
    tm = next((t for t in (512, SEQ_ALIGN) if seq % t == 0), None)
    if tm is None:
        if heads is not None:
            h = _dsa_out(h, *heads)
        return _ffn(h.reshape(b * tp, d), g, w_in, w_out).reshape(b, tp, d)[:, N_META:N_META + seq]
    win, wout = w_in.astype(_MXU), w_out.astype(_MXU)
    nxt = lambda bi, i: (bi, (i + 1) * (tm // N_META), 0)
    acts = [h, h]
    specs = [pl.BlockSpec((1, tm, d), lambda bi, i: (bi, i, 0)), pl.BlockSpec((1, N_META, d), nxt)]
    if heads is not None:
        y, w_o = heads
        acts += [y, y, w_o.astype(_MXU)]
        specs += [pl.BlockSpec((1, ATT_HEADS, tm, HEAD_DIM), lambda bi, i: (bi, 0, i, 0)),
                  pl.BlockSpec((1, ATT_HEADS, N_META, HEAD_DIM), lambda bi, i: (bi, 0, (i + 1) * (tm // N_META), 0)),
                  _const_spec(w_o.shape)]
    return pl.pallas_call(
        _ffn_tail_body if heads is None else _ffn_tail_heads_body,
        out_shape=jax.ShapeDtypeStruct((b, seq, d), _F32),
        grid=(b, seq // tm),
        in_specs=specs + [_const_spec((1, d)), _const_spec(win.shape), _const_spec(wout.shape)],
        out_specs=pl.BlockSpec((1, tm, d), lambda bi, i: (bi, i, 0)),
        compiler_params=_params("parallel", "parallel"),
        name="ffn_tail",
    )(*acts, g.reshape(1, d), win, wout)


def _rwkv_prep_body(h_ref, hprev_ref, ng_ref, mix_ref, wr_ref, wk_ref, wv_ref, w1_ref, w2_ref,
                    a1_ref, a2_ref, g1_ref, g2_ref, w0_ref, a0_ref,
                    r_ref, k_ref, v_ref, lw_ref, a_ref, g_ref):
    ng = ng_ref[...]
    hn = _rms(h_ref[0]) * ng
    prev = (_rms(hprev_ref[0]) * ng)[7:8, :]
    prev = jnp.where(pl.program_id(1) == 0, 0.0, prev)
    row = lax.broadcasted_iota(jnp.int32, hn.shape, 0)
    shifted = jnp.where(row == 0, prev, pltpu.roll(hn, 1, 0))
    xx = shifted - hn

    def mixed(j):
        return (hn + xx * mix_ref[j:j + 1, :]).astype(_MXU)

    r_ref[0] = _dot(mixed(0), wr_ref[...])
    k_ref[0] = _dot(mixed(2), wk_ref[...])
    v_ref[0] = _dot(mixed(3), wv_ref[...])
    wl = w0_ref[...] + _dot(jnp.tanh(_dot(mixed(1), w1_ref[...])), w2_ref[...])
    lw_ref[0] = -0.6065306597126334 * _sigmoid(wl)
    a_ref[0] = _sigmoid(a0_ref[...] + _dot(_dot(mixed(4), a1_ref[...]), a2_ref[...]))
    g_ref[0] = _dot(_sigmoid(_dot(mixed(5), g1_ref[...])), g2_ref[...])


def _rwkv_prep(h, ng, mix, w_r, w_k, w_v, w1, w2, a1, a2, g1, g2, w0, a0):
    b, tp, d = h.shape
    tm = SEQ_ALIGN
    ws = [w.astype(_MXU) for w in (w_r, w_k, w_v, w1, w2, a1, a2, g1, g2)]
    act = pl.BlockSpec((1, tm, d), lambda bi, i: (bi, i, 0))
    prev = pl.BlockSpec((1, 8, d), lambda bi, i: (bi, jnp.maximum(i * (tm // 8) - 1, 0), 0))
    out = jax.ShapeDtypeStruct((b, tp, d), _F32)
    return pl.pallas_call(
        _rwkv_prep_body,
        out_shape=[out] * 6,
        grid=(b, tp // tm),
        in_specs=[act, prev, _const_spec((1, d)), _const_spec(mix.shape)]
                 + [_const_spec(w.shape) for w in ws]
                 + [_const_spec((1, d)), _const_spec((1, d))],
        out_specs=[act] * 6,
        compiler_params=_params("parallel", "parallel"),
        name="rwkv_prep",
    )(h, h, ng.reshape(1, d), mix, *ws, w0.reshape(1, d), a0.reshape(1, d))


def _wkv_body(r_ref, k_ref, v_ref, lw_ref, a_ref, kk_ref, ka_ref, rk_ref, lg_ref, lb_ref,
              o_ref, s_ref, *, n_chunks, n_pairs):
    c_len = WKV_CHUNK

    @pl.when(pl.program_id(2) == 0)
    def _():
        s_ref[...] = jnp.zeros(s_ref.shape, _F32)

    lane = lax.broadcasted_iota(jnp.int32, (c_len, LANES), 1)
    row = lax.broadcasted_iota(jnp.int32, (c_len, LANES), 0)
    first = lane < HEAD_DIM
    col = jnp.where(first, lane, lane - HEAD_DIM)
    strict = col < row
    incl = col <= row
    sq_r = lax.broadcasted_iota(jnp.int32, (LANES, LANES), 0)
    sq_c = lax.broadcasted_iota(jnp.int32, (LANES, LANES), 1)
    same_head = (sq_r < HEAD_DIM) == (sq_c < HEAD_DIM)
    eye = sq_r == sq_c
    tri_r = lax.broadcasted_iota(jnp.int32, (c_len, c_len), 0)
    tri_c = lax.broadcasted_iota(jnp.int32, (c_len, c_len), 1)
    tri = jnp.where(tri_c <= tri_r, 1.0, 0.0).astype(_MXU)

    def head_sum(x):
        s0 = jnp.sum(jnp.where(first, x, 0.0), axis=-1, keepdims=True)
        s1 = jnp.sum(jnp.where(first, 0.0, x), axis=-1, keepdims=True)
        return jnp.where(first, s0, s1)

    def blockdiag(x):
        f = first if x.shape[1] == LANES else jnp.concatenate([first] * (x.shape[1] // LANES), axis=1)
        xb = x.astype(_MXU)
        zero = jnp.zeros_like(xb)
        return jnp.concatenate([jnp.where(f, xb, zero), jnp.where(f, zero, xb)], axis=0)

    def apply(p, x):
        return jnp.dot(p.astype(_MXU), blockdiag(x), preferred_element_type=_F32)

    units = [(c, p) for c in range(n_chunks) for p in range(n_pairs)]
    st = {}
    for u in units:
        c, p = u
        sl = pl.ds(c * c_len, c_len)
        ln = slice(p * LANES, (p + 1) * LANES)
        r = r_ref[0, sl, ln]
        k_raw = k_ref[0, sl, ln]
        v = v_ref[0, sl, ln]
        lw = lw_ref[0, sl, ln]
        gate = a_ref[0, sl, ln]
        kk = k_raw * kk_ref[:, ln]
        kk = kk / jnp.maximum(jnp.sqrt(head_sum(kk * kk)), 1e-12)
        k = k_raw * (1.0 + (gate - 1.0) * ka_ref[:, ln])
        h3 = _split3(lw)
        cs = (jnp.dot(tri, h3[0], preferred_element_type=_F32)
              + jnp.dot(tri, h3[1], preferred_element_type=_F32)
              + jnp.dot(tri, h3[2], preferred_element_type=_F32))
        st[u] = dict(r=r, k=k, v=v, lw=lw, cs=cs, a_vec=-kk, b_vec=kk * gate)

    for u in units:
        d = st[u]
        cs, lw = d["cs"], d["lw"]
        cs_prev = cs - lw
        c_mid = cs[c_len // 2 - 1:c_len // 2, :]
        c_end = cs[c_len - 1:c_len, :]
        inv_rel = jnp.exp(c_mid - cs)
        to_end = jnp.exp(c_end - cs)
        d["a_abs"] = d["a_vec"] * jnp.exp(cs_prev)
        d["r_abs"] = d["r"] * jnp.exp(cs)
        d["b_end"] = d["b_vec"] * to_end
        d["k_end"] = d["k"] * to_end
        d["g_end"] = jnp.exp(c_end)
        lhs = jnp.concatenate([d["a_vec"] * jnp.exp(cs_prev - c_mid), d["r"] * jnp.exp(cs - c_mid)],
                              axis=0).astype(_MXU)
        d["sb"] = lax.dot_general(lhs, blockdiag(d["b_vec"] * inv_rel), (((1,), (1,)), ((), ())),
                                  preferred_element_type=_F32)
        d["sk"] = lax.dot_general(lhs, blockdiag(d["k"] * inv_rel), (((1,), (1,)), ((), ())),
                                  preferred_element_type=_F32)

    base_blk = (row // WKV_INV_BASE) == (col // WKV_INV_BASE)
    for u in units:
        d = st[u]
        d["l"] = jnp.where(strict, d["sb"][:c_len], 0.0)
        d["g_rb"] = jnp.where(incl, d["sb"][c_len:], 0.0)
        d["g_rk"] = jnp.where(incl, d["sk"][c_len:], 0.0)
        l_ak = jnp.where(strict, d["sk"][:c_len], 0.0)
        d["x0"] = jnp.concatenate([d["a_abs"], apply(l_ak, d["v"])], axis=1)
        l_base = jnp.where(base_blk, d["l"], 0.0)
        d["t"] = jnp.where(col == row, 1.0, l_base)
        d["q"] = apply(l_base, l_base)
    for u in units:
        d = st[u]
        d["t"] = d["t"] + apply(d["t"], d["q"])
    width = WKV_INV_BASE
    while width < c_len:
        off_blk = ((row // (2 * width)) == (col // (2 * width))) & ((row // width) != (col // width))
        for u in units:
            d = st[u]
            d["q"] = apply(jnp.where(off_blk, d["l"], 0.0), d["t"])
        for u in units:
            d = st[u]
            d["t"] = d["t"] + apply(d["t"], d["q"])
        width *= 2
    for u in units:
        d = st[u]
        d["x"] = apply(d["t"], d["x0"])

    for u in units:
        d = st[u]
        x = d["x"]
        w_a = x[:, :LANES]
        u0 = x[:, LANES:]
        gx = apply(d["g_rb"], x)
        d["q_eff"] = d["r_abs"] + gx[:, :LANES]
        d["y0"] = gx[:, LANES:] + apply(d["g_rk"], d["v"])
        d["m"] = jnp.where(same_head, _dot_tn(w_a, d["b_end"]), 0.0) + jnp.where(eye, d["g_end"], 0.0)
        d["n"] = jnp.where(same_head, _dot_tn(u0, d["b_end"]) + _dot_tn(d["v"], d["k_end"]), 0.0)

    state = [s_ref[p] for p in range(n_pairs)]
    for u in units:
        c, p = u
        d = st[u]
        s_hi, s_lo = _split2(state[p])
        d["y"] = d["y0"] + _dot_nt(d["q_eff"], s_hi) + _dot_nt(d["q_eff"], s_lo)
        m_hi, m_lo = _split2(d["m"])
        state[p] = (jnp.dot(s_hi, m_hi, preferred_element_type=_F32)
                    + jnp.dot(s_hi, m_lo, preferred_element_type=_F32)
                    + jnp.dot(s_lo, m_hi, preferred_element_type=_F32) + d["n"])
    for p in range(n_pairs):
        s_ref[p] = state[p]

    for u in units:
        c, p = u
        d = st[u]
        ln = slice(p * LANES, (p + 1) * LANES)
        y = d["y"]
        mu = head_sum(y) * (1.0 / HEAD_DIM)
        yc = y - mu
        var = head_sum(yc * yc) * (1.0 / HEAD_DIM)
        yn = yc * lax.rsqrt(var + RWKV_LN_EPS) * lg_ref[:, ln] + lb_ref[:, ln]
        o_ref[0, pl.ds(c * c_len, c_len), ln] = yn + head_sum(d["r"] * d["k"] * rk_ref[:, ln]) * d["v"]


def _wkv(r, k, v, lw, a, k_k, k_a, r_k, ln_g, ln_b):
    b, tp, d = r.shape
    tc = WKV_CHUNK * WKV_CHUNKS_PER_STEP
    width = LANES * WKV_PAIRS_PER_STEP
    act = pl.BlockSpec((1, tc, width), lambda bi, p, s: (bi, s, p))
    par = pl.BlockSpec((1, width), lambda bi, p, s: (0, p))
    return pl.pallas_call(
        functools.partial(_wkv_body, n_chunks=WKV_CHUNKS_PER_STEP, n_pairs=WKV_PAIRS_PER_STEP),
        out_shape=jax.ShapeDtypeStruct((b, tp, d), _F32),
        grid=(b, d // width, tp // tc),
        in_specs=[act] * 5 + [par] * 5,
        out_specs=act,
        scratch_shapes=[pltpu.VMEM((WKV_PAIRS_PER_STEP, LANES, LANES), _F32)],
        compiler_params=_params("parallel", "parallel", "arbitrary"),
        name="wkv",
    )(r, k, v, lw, a, *(p.reshape(1, d) for p in (k_k, k_a, r_k, ln_g, ln_b)))


def _gated_out_body(h_ref, y_ref, g_ref, w_ref, o_ref):
    o_ref[...] = h_ref[...] + _dot(y_ref[...] * g_ref[...], w_ref[...])


def _gated_out(h2, y2, gate2, w):
    rows, d = h2.shape
    tm = 512 if rows % 512 == 0 else SEQ_ALIGN
    tile = pl.BlockSpec((tm, d), lambda i: (i, 0))
    return pl.pallas_call(
        _gated_out_body,
        out_shape=jax.ShapeDtypeStruct((rows, d), _F32),
        grid=(rows // tm,),
        in_specs=[tile] * 3 + [_const_spec(w.shape)],
        out_specs=tile,
        compiler_params=_params("parallel"),
        name="rwkv_out",
    )(h2, y2, gate2, w.astype(_MXU))


def _rwkv_mix(h, ng, p):
    r, k, v, lw, a, g = _rwkv_prep(h, ng, p["mix"], p["w_r"], p["w_k"], p["w_v"], p["w1"], p["w2"],
                                   p["a1"], p["a2"], p["g1"], p["g2"], p["w0"], p["a0"])
    return _wkv(r, k, v, lw, a, p["k_k"], p["k_a"], p["r_k"], p["lnx_g"], p["lnx_b"]), g


def _rwkv_layer(h, ng, p):
    b, tp, d = h.shape
    y, g = _rwkv_mix(h, ng, p)
    flat = lambda t: t.reshape(b * tp, d)
    return _gated_out(flat(h), flat(y), flat(g), p["w_o"]).reshape(b, tp, d)


def _rope_tables(tp):
    half = ROPE_DIM // 2
    inv = ROPE_THETA ** (-jnp.arange(0, ROPE_DIM, 2, dtype=_F32) / ROPE_DIM)
    ang = jnp.arange(tp, dtype=_F32)[:, None] * inv[None, :]
    cos, sin = jnp.cos(ang), jnp.sin(ang)
    ones = jnp.ones((tp, HEAD_DIM - ROPE_DIM), _F32)
    zeros = jnp.zeros((tp, HEAD_DIM - ROPE_DIM), _F32)
    zh = jnp.zeros((tp, half), _F32)
    c = jnp.concatenate([cos, cos, ones], axis=1)
    s_up = jnp.concatenate([-sin, zh, zeros], axis=1)
    s_dn = jnp.concatenate([zh, sin, zeros], axis=1)
    return tuple(jnp.concatenate([t, t], axis=1) for t in (c, s_up, s_dn))


def _dsa_proj_body(h_ref, ng_ref, wq_ref, wk_ref, wv_ref, wqi_ref, ws_ref, qg_ref, kg_ref,
                   c_ref, su_ref, sd_ref, q_ref, k_ref, v_ref, qi_ref, ki_ref, wi_ref):
    hb = (_rms(h_ref[0]) * ng_ref[...]).astype(_MXU)
    c, s_up, s_dn = c_ref[...], su_ref[...], sd_ref[...]
    lane = lax.broadcasted_iota(jnp.int32, c.shape, 1)
    first = lane < HEAD_DIM

    def rope(x):
        return x * c + pltpu.roll(x, LANES - ROPE_DIM // 2, 1) * s_up + pltpu.roll(x, ROPE_DIM // 2, 1) * s_dn

    sq_r = lax.broadcasted_iota(jnp.int32, (LANES, LANES), 0)
    sq_c = lax.broadcasted_iota(jnp.int32, (LANES, LANES), 1)
    head_ones = jnp.where((sq_r < HEAD_DIM) == (sq_c < HEAD_DIM), 1.0, 0.0).astype(_MXU)

    def head_rms(x, gain):
        hi, lo = _split2(x * x)
        ss = (jnp.dot(hi, head_ones, preferred_element_type=_F32)
              + jnp.dot(lo, head_ones, preferred_element_type=_F32))
        return x * lax.rsqrt(ss * (1.0 / HEAD_DIM) + NORM_EPS) * gain

    def put_heads(dst_ref, pair, x):
        dst_ref[0, 2 * pair] = x[:, :HEAD_DIM].astype(dst_ref.dtype)
        dst_ref[0, 2 * pair + 1] = x[:, HEAD_DIM:].astype(dst_ref.dtype)

    def with_ones(x):
        return jnp.where(first, x, jnp.where(lane == HEAD_DIM, 1.0, 0.0)).astype(v_ref.dtype)

    q = _dot(hb, wq_ref[...])
    for p in range(ATT_Q_W // LANES):
        blk = rope(head_rms(q[:, p * LANES:(p + 1) * LANES], qg_ref[...]))
        put_heads(q_ref, p, blk * (HEAD_DIM ** -0.5 * LOG2_E))
    k = _dot(hb, wk_ref[...])
    v = _dot(hb, wv_ref[...])
    for p in range(ATT_KV_W // LANES):
        put_heads(k_ref, p, rope(head_rms(k[:, p * LANES:(p + 1) * LANES], kg_ref[...])))
        vp = v[:, p * LANES:(p + 1) * LANES]
        v_ref[0, 2 * p] = with_ones(vp)
        v_ref[0, 2 * p + 1] = with_ones(pltpu.roll(vp, HEAD_DIM, 1))
    qi = _dot(hb, wqi_ref[...])
    for p in range(IDX_Q_W // LANES):
        put_heads(qi_ref, p, rope(qi[:, p * LANES:(p + 1) * LANES]))
    small = _dot(hb, ws_ref[...])
    ki_ref[0] = rope(small)[:, :IDX_DIM].astype(ki_ref.dtype)
    wi_ref[0] = small[:, IDX_DIM:IDX_DIM + IDX_HEADS] * (IDX_HEADS ** -0.5 * IDX_DIM ** -0.5)


def _dsa_proj(h, ng, w_in, q_g, k_g):
    b, tp, d = h.shape
    tm = SEQ_ALIGN
    o0, o1, o2, o3, o4 = (0, ATT_Q_W, ATT_Q_W + ATT_KV_W, ATT_Q_W + 2 * ATT_KV_W,
                          ATT_Q_W + 2 * ATT_KV_W + IDX_Q_W)
    w = w_in.astype(_MXU)
    w_small = jnp.pad(w[:, o4:], ((0, 0), (0, LANES - (w.shape[1] - o4))))
    ws = [w[:, o0:o1], w[:, o1:o2], w[:, o2:o3], w[:, o3:o4], w_small]
    tables = _rope_tables(tp)
    pair = lambda gvec: jnp.concatenate([gvec, gvec]).reshape(1, LANES)
    heads = lambda n: (jax.ShapeDtypeStruct((b, n, tp, HEAD_DIM), _MXU),
                       pl.BlockSpec((1, n, tm, HEAD_DIM), lambda bi, i: (bi, 0, i, 0)))
    outs = [heads(ATT_HEADS), heads(ATT_KV_HEADS),
            (jax.ShapeDtypeStruct((b, ATT_KV_HEADS, tp, LANES), _MXU),
             pl.BlockSpec((1, ATT_KV_HEADS, tm, LANES), lambda bi, i: (bi, 0, i, 0))),
            heads(IDX_HEADS),
            (jax.ShapeDtypeStruct((b, tp, IDX_DIM), _MXU), pl.BlockSpec((1, tm, IDX_DIM), lambda bi, i: (bi, i, 0))),
            (jax.ShapeDtypeStruct((b, tp, IDX_HEADS), _F32), pl.BlockSpec((1, tm, IDX_HEADS), lambda bi, i: (bi, i, 0)))]
    tab = pl.BlockSpec((tm, LANES), lambda bi, i: (i, 0))
    return pl.pallas_call(
        _dsa_proj_body,
        out_shape=[o[0] for o in outs],
        grid=(b, tp // tm),
        in_specs=[pl.BlockSpec((1, tm, d), lambda bi, i: (bi, i, 0)), _const_spec((1, d))]
                 + [_const_spec(x.shape) for x in ws]
                 + [_const_spec((1, LANES)), _const_spec((1, LANES)), tab, tab, tab],
        out_specs=[o[1] for o in outs],
        compiler_params=_params("parallel", "parallel"),
        name="dsa_proj",
    )(h, ng.reshape(1, d), *ws, pair(q_g), pair(k_g), *tables)


def _dsa_attn_body(qi_ref, wi_ref, ki_ref, q_ref, k_ref, v_ref, o_ref,
                   keys_ref, bias_ref, wb_ref, m_ref, acc_ref, thr_ref, cnt_ref, *, topk, widths):
    tq = SEQ_ALIGN
    tk = SEQ_ALIGN
    i = pl.program_id(1)
    n_tiles = i + 1
    n_key_tiles = keys_ref.shape[1] // tk

    keys_ref[...] = jnp.full(keys_ref.shape, INT_MIN, jnp.int32)
    wi = wi_ref[0]
    for hd in range(IDX_HEADS):
        wb_ref[hd] = jnp.broadcast_to(wi[:, hd:hd + 1], (tq, LANES))
    qi = qi_ref[0].reshape(IDX_HEADS * tq, IDX_DIM)
    row = lax.broadcasted_iota(jnp.int32, (tq, tk), 0)
    col = lax.broadcasted_iota(jnp.int32, (tq, tk), 1)

    def score_tiles(starts):
        zs = [_dot_nt(qi, ki_ref[0, pl.ds(start, tk), :]) for start in starts]
        for start, z in zip(starts, zs):
            score = jnp.zeros((tq, tk), _F32)
            for hd in range(IDX_HEADS):
                wb = wb_ref[hd]
                score = score + jnp.maximum(z[hd * tq:(hd + 1) * tq], 0.0) * jnp.concatenate([wb, wb], axis=1)
            bits = pltpu.bitcast(score, jnp.int32)
            key = bits ^ ((bits >> 31) & 0x7FFFFFFF)
            key = jnp.where(bits == INT_MIN, 0, key)
            key = jnp.maximum(key, INT_MIN + 1)
            causal = (start + col) <= (i * tq + row)
            keys_ref[:, pl.ds(start, tk)] = jnp.where(causal, key, INT_MIN)

    def score_pair(j, carry):
        start = pl.multiple_of(j * (2 * tk), 2 * tk)
        score_tiles([start, start + tk])
        return carry

    lax.fori_loop(0, n_tiles // 2, score_pair, 0)

    @pl.when(n_tiles % 2 == 1)
    def _():
        score_tiles([pl.multiple_of(i * tk, tk)])

    def count(mask):
        return jnp.sum(mask.astype(jnp.int32), axis=-1, keepdims=True)

    def select(width, cover):
        kv = keys_ref[:, :width]

        def value_bit(it, thr):
            cand = thr + (jnp.int32(1) << (31 - it))
            return jnp.where(count(kv >= cand) >= topk, cand, thr)

        thr = lax.fori_loop(0, 32, value_bit, jnp.full((tq, 1), INT_MIN, jnp.int32))
        floor = jnp.maximum(thr, INT_MIN + 1)
        picked = kv >= floor
        bias_ref[:, :width] = jnp.where(picked, 0.0, NEG_BIG)
        if cover > width:
            bias_ref[:, width:cover] = jnp.full((tq, cover - width), NEG_BIG, _F32)
        thr_ref[...] = thr
        cnt_ref[...] = count(picked)

    lo = 0
    for n_q, width, cover in widths:
        @pl.when((i >= lo) & (i < n_q))
        def _(width=width, cover=cover):
            select(width, cover)
        lo = n_q

    @pl.when(jnp.max(cnt_ref[...]) > topk)
    def _():
        kv = keys_ref[...]
        thr = thr_ref[...]
        above = kv > thr
        equal = (kv == thr) & (thr > INT_MIN)
        need = topk - count(above)
        pos = lax.broadcasted_iota(jnp.int32, kv.shape, 1)

        def pos_bit(it, last_short):
            cand = last_short + (jnp.int32(1) << (12 - it))
            return jnp.where(count(equal & (pos <= cand)) < need, cand, last_short)

        limit = lax.fori_loop(0, 13, pos_bit, jnp.full((tq, 1), -1, jnp.int32)) + 1
        bias_ref[...] = jnp.where(above | (equal & (pos <= limit)), 0.0, NEG_BIG)

    rows = ATT_GROUP * ATT_ROWS
    tiles_per_span = ATT_SPAN // tk
    span_tiles = (n_key_tiles // tiles_per_span) * tiles_per_span
    n_spans = (jnp.minimum(n_tiles, span_tiles) + tiles_per_span - 1) // tiles_per_span
    chains = [(gl, qh) for gl in range(ATT_GROUPS_PER_STEP) for qh in range(tq // ATT_ROWS)]

    def group_step(gs, carry):
        kv = [gs * ATT_GROUPS_PER_STEP + gl for gl, _ in chains]
        q_rows = [slice(qh * ATT_ROWS, (qh + 1) * ATT_ROWS) for _, qh in chains]
        qs = [q_ref[0, pl.ds(kv[c] * ATT_GROUP, ATT_GROUP), q_rows[c], :].reshape(rows, HEAD_DIM)
              for c in range(len(chains))]
        m_ref[...] = jnp.full(m_ref.shape, NEG_BIG, _F32)
        acc_ref[...] = jnp.zeros(acc_ref.shape, _F32)

        def flash(start, width):
            s = [_dot_nt(qs[c], k_ref[0, kv[c], pl.ds(start, width), :]) for c in range(len(chains))]
            p = []
            for c in range(len(chains)):
                bias = bias_ref[q_rows[c], pl.ds(start, width)]
                sc = (s[c].reshape(ATT_GROUP, ATT_ROWS, width) + bias[None]).reshape(rows, width)
                m_prev = m_ref[c]
                m_new = jnp.maximum(m_prev, jnp.max(sc, axis=-1, keepdims=True))
                acc_ref[c] = jnp.exp2(m_prev - m_new) * acc_ref[c]
                m_ref[c] = m_new
                p.append(jnp.exp2(sc - m_new).astype(_MXU))
            for c in range(len(chains)):
                acc_ref[c] += jnp.dot(p[c], v_ref[0, kv[c], pl.ds(start, width), :],
                                      preferred_element_type=_F32)

        def span_step(j, carry):
            flash(pl.multiple_of(j * ATT_SPAN, ATT_SPAN), ATT_SPAN)
            return carry

        lax.fori_loop(0, n_spans, span_step, 0)
        if n_key_tiles > span_tiles:
            @pl.when(n_tiles > span_tiles)
            def _():
                flash(span_tiles * tk, (n_key_tiles - span_tiles) * tk)
        for c in range(len(chains)):
            acc = acc_ref[c]
            out = acc[:, :HEAD_DIM] / acc[:, HEAD_DIM:HEAD_DIM + 1]
            o_ref[0, pl.ds(kv[c] * ATT_GROUP, ATT_GROUP), q_rows[c], :] = (
                out.reshape(ATT_GROUP, ATT_ROWS, HEAD_DIM).astype(o_ref.dtype))
        return carry

    lax.fori_loop(0, ATT_KV_HEADS // ATT_GROUPS_PER_STEP, group_step, 0)


def _dsa_attn(q, k, v, qi, ki, wi, topk):
    b, _, tp, _ = q.shape
    tq = SEQ_ALIGN
    nq = tp // tq
    tiles_per_span = ATT_SPAN // tq
    classes = sorted(set(range(SELECT_CLASS_TILES, nq + 1, SELECT_CLASS_TILES)) | {nq})
    widths = tuple((n, n * tq, min(-(-n // tiles_per_span) * tiles_per_span, nq) * tq) for n in classes)
    n_chains = ATT_GROUPS_PER_STEP * (tq // ATT_ROWS)
    per_q = lambda n: pl.BlockSpec((1, n, tq, HEAD_DIM), lambda bi, i: (bi, 0, i, 0))
    per_b = lambda n, w: pl.BlockSpec((1, n, tp, w), lambda bi, i: (bi, 0, 0, 0),
                                      pipeline_mode=pl.Buffered(1))
    return pl.pallas_call(
        functools.partial(_dsa_attn_body, topk=topk, widths=widths),
        out_shape=jax.ShapeDtypeStruct((b, ATT_HEADS, tp, HEAD_DIM), _MXU),
        grid=(b, nq),
        in_specs=[per_q(IDX_HEADS),
                  pl.BlockSpec((1, tq, IDX_HEADS), lambda bi, i: (bi, i, 0)),
                  pl.BlockSpec((1, tp, IDX_DIM), lambda bi, i: (bi, 0, 0), pipeline_mode=pl.Buffered(1)),
                  per_q(ATT_HEADS), per_b(ATT_KV_HEADS, HEAD_DIM), per_b(ATT_KV_HEADS, LANES)],
        out_specs=per_q(ATT_HEADS),
        scratch_shapes=[pltpu.VMEM((tq, tp), jnp.int32),
                        pltpu.VMEM((tq, tp), _F32),
                        pltpu.VMEM((IDX_HEADS, tq, LANES), _F32),
                        pltpu.VMEM((n_chains, ATT_GROUP * ATT_ROWS, 1), _F32),
                        pltpu.VMEM((n_chains, ATT_GROUP * ATT_ROWS, LANES), _F32),
                        pltpu.VMEM((tq, 1), jnp.int32),
                        pltpu.VMEM((tq, 1), jnp.int32)],
        compiler_params=_params("parallel", "arbitrary"),
        name="dsa_attn",
    )(qi, wi, ki, q, k, v)


def _dsa_out_body(h_ref, y_ref, w_ref, o_ref):
    y = jnp.concatenate([y_ref[0, hd] for hd in range(ATT_HEADS)], axis=1)
    o_ref[0] = h_ref[0] + _dot(y, w_ref[...])


def _dsa_out(h, y, w_o):
    b, tp, d = h.shape
    tm = SEQ_ALIGN
    tile = pl.BlockSpec((1, tm, d), lambda bi, i: (bi, i, 0))
    return pl.pallas_call(
        _dsa_out_body,
        out_shape=jax.ShapeDtypeStruct((b, tp, d), _F32),
        grid=(b, tp // tm),
        in_specs=[tile, pl.BlockSpec((1, ATT_HEADS, tm, HEAD_DIM), lambda bi, i: (bi, 0, i, 0)),
                  _const_spec(w_o.shape)],
        out_specs=tile,
        compiler_params=_params("parallel", "parallel"),
        name="dsa_out",
    )(h, y, w_o.astype(_MXU))


def _dsa_mix(h, ng, w_in, q_g, k_g, topk):
    q, k, v, qi, ki, wi = _dsa_proj(h, ng, w_in, q_g, k_g)
    return _dsa_attn(q, k, v, qi, ki, wi, topk)


def _dsa_layer(h, ng, w_in, q_g, k_g, w_o, topk):
    return _dsa_out(h, _dsa_mix(h, ng, w_in, q_g, k_g, topk), w_o)


def kernel(x, meta, norm_g, ffn_w_in, ffn_w_out, rk_mix, rk_w_r, rk_w_k, rk_w_v, rk_w_o, rk_w0, rk_w1,
           rk_w2, rk_a0, rk_a1, rk_a2, rk_g1, rk_g2, rk_k_k, rk_k_a, rk_r_k, rk_lnx_g, rk_lnx_b,
           at_w_in, at_q_g, at_k_g, at_w_o):
    b, seq, d = x.shape
    topk = min(TOPK_MAX, seq // 4)
    t = seq + N_META
    tp = -(-t // SEQ_ALIGN) * SEQ_ALIGN
    h = jnp.concatenate([jnp.broadcast_to(meta.astype(x.dtype)[None], (b, N_META, d)), x,
                         jnp.zeros((b, tp - t, d), x.dtype)], axis=1)
    rk = dict(mix=rk_mix, w_r=rk_w_r, w_k=rk_w_k, w_v=rk_w_v, w_o=rk_w_o, w0=rk_w0, w1=rk_w1, w2=rk_w2,
              a0=rk_a0, a1=rk_a1, a2=rk_a2, g1=rk_g1, g2=rk_g2, k_k=rk_k_k, k_a=rk_k_a, r_k=rk_r_k,
              lnx_g=rk_lnx_g, lnx_b=rk_lnx_b)
    n_mixers = 2
    depth = norm_g.shape[0]
    flat = lambda a: a.reshape(b * tp, d)
    for layer in range(depth):
        g = norm_g[layer]
        last = layer + 1 == depth
        w_in2, w_out2 = ffn_w_in[layer, 1], ffn_w_out[layer, 1]
        h = _ffn(flat(h), g[0], ffn_w_in[layer, 0], ffn_w_out[layer, 0]).reshape(b, tp, d)
        j = layer // n_mixers
        if layer % n_mixers == 0:
            p = {name: val[j] for name, val in rk.items()}
            if last:
                out = _ffn_tail(_rwkv_layer(h, g[1], p), g[2], w_in2, w_out2, seq)
            else:
                y, gate = _rwkv_mix(h, g[1], p)
                h = _ffn(flat(h), g[2], w_in2, w_out2, gated=(flat(y), flat(gate), p["w_o"])).reshape(b, tp, d)
        else:
            if last:
                y = _dsa_mix(h, g[1], at_w_in[j], at_q_g[j], at_k_g[j], topk)
                out = _ffn_tail(h, g[2], w_in2, w_out2, seq, heads=(y, at_w_o[j]))
            else:
                h = _dsa_layer(h, g[1], at_w_in[j], at_q_g[j], at_k_g[j], at_w_o[j], topk)
                h = _ffn(flat(h), g[2], w_in2, w_out2).reshape(b, tp, d)
    return out
```

```python
import functools

import jax
import jax.numpy as jnp
from jax import lax
from jax.experimental import pallas as pl
from jax.experimental.pallas import tpu as pltpu

D_MODEL = 1024
N_META = 16
D_FF = 2816
NORM_EPS = 1e-6
HEAD_DIM = 64
ROPE_DIM = HEAD_DIM // 4
ROPE_THETA = 500000.0
RWKV_HEADS = D_MODEL // HEAD_DIM
RWKV_LN_EPS = 64e-5
ATT_HEADS = D_MODEL // HEAD_DIM
ATT_KV_HEADS = 4
ATT_GROUP = ATT_HEADS // ATT_KV_HEADS
IDX_HEADS = 8
IDX_DIM = 64
TOPK_MAX = 256
ATT_Q_W = ATT_HEADS * HEAD_DIM
ATT_KV_W = ATT_KV_HEADS * HEAD_DIM
IDX_Q_W = IDX_HEADS * IDX_DIM

LANES = 128
SEQ_ALIGN = 256
VMEM_LIMIT = 56 * 1024 * 1024
FF_CHUNK = 256
WKV_CHUNK = 64
WKV_CHUNKS_PER_STEP = 4
WKV_INV_BASE = 4
WKV_PAIRS_PER_STEP = 8
ATT_ROWS = 128
ATT_SPAN = 1024
ATT_GROUPS_PER_STEP = 2
SELECT_CLASS_TILES = 2
INT_MIN = -(2 ** 31)
NEG_BIG = -1e30
LOG2_E = 1.4426950408889634

_MXU = jnp.bfloat16
_F32 = jnp.float32


def _dot(a, b):
    return jnp.dot(a.astype(_MXU), b.astype(_MXU), preferred_element_type=_F32)


def _dot_nt(a, b):
    return lax.dot_general(a.astype(_MXU), b.astype(_MXU), (((1,), (1,)), ((), ())),
                           preferred_element_type=_F32)


def _dot_tn(a, b):
    return lax.dot_general(a.astype(_MXU), b.astype(_MXU), (((0,), (0,)), ((), ())),
                           preferred_element_type=_F32)


def _split2(x):
    hi = x.astype(_MXU)
    lo = (x - hi.astype(_F32)).astype(_MXU)
    return hi, lo


def _split3(x):
    hi = x.astype(_MXU)
    r1 = x - hi.astype(_F32)
    mid = r1.astype(_MXU)
    lo = (r1 - mid.astype(_F32)).astype(_MXU)
    return hi, mid, lo


def _rms(x, eps=NORM_EPS):
    return x * lax.rsqrt(jnp.mean(x * x, axis=-1, keepdims=True) + eps)


def _sigmoid(x):
    return 1.0 / (1.0 + jnp.exp(-x))


def _params(*sem):
    return pltpu.CompilerParams(dimension_semantics=sem, vmem_limit_bytes=VMEM_LIMIT)


def _const_spec(shape):
    n = len(shape)
    return pl.BlockSpec(shape, lambda *_: (0,) * n, pipeline_mode=pl.Buffered(1))


def _swiglu_residual(x, g_ref, win_ref, wout_ref):
    xb = (_rms(x) * g_ref[...]).astype(_MXU)
    acc = jnp.zeros(x.shape, _F32)
    for c in range(D_FF // FF_CHUNK):
        lo = c * FF_CHUNK
        gate = _dot(xb, win_ref[:, lo:lo + FF_CHUNK])
        up = _dot(xb, win_ref[:, D_FF + lo:D_FF + lo + FF_CHUNK])
        act = gate * _sigmoid(gate) * up
        acc = acc + _dot(act, wout_ref[lo:lo + FF_CHUNK, :])
    return x + 0.5 * acc


def _ffn_body(h_ref, g_ref, win_ref, wout_ref, o_ref):
    o_ref[...] = _swiglu_residual(h_ref[...], g_ref, win_ref, wout_ref)


def _ffn_gated_body(h_ref, y_ref, gate_ref, wo_ref, g_ref, win_ref, wout_ref, o_ref):
    x = h_ref[...] + _dot(y_ref[...] * gate_ref[...], wo_ref[...])
    o_ref[...] = _swiglu_residual(x, g_ref, win_ref, wout_ref)


def _shifted(ref, next_ref):
    return jnp.concatenate([ref[N_META:, :], next_ref[...]], axis=0)


def _ffn_tail_body(h_ref, hn_ref, g_ref, win_ref, wout_ref, o_ref):
    x = _shifted(h_ref.at[0], hn_ref.at[0])
    o_ref[0] = _swiglu_residual(x, g_ref, win_ref, wout_ref)


def _ffn_tail_heads_body(h_ref, hn_ref, y_ref, yn_ref, wo_ref, g_ref, win_ref, wout_ref, o_ref):
    y = jnp.concatenate([_shifted(y_ref.at[0, hd], yn_ref.at[0, hd]) for hd in range(ATT_HEADS)], axis=1)
    x = _shifted(h_ref.at[0], hn_ref.at[0]) + _dot(y, wo_ref[...])
    o_ref[0] = _swiglu_residual(x, g_ref, win_ref, wout_ref)


def _ffn(h2, g, w_in, w_out, gated=None):
    rows, d = h2.shape
    tm = 512 if rows % 512 == 0 else SEQ_ALIGN
    win, wout = w_in.astype(_MXU), w_out.astype(_MXU)
    tile = pl.BlockSpec((tm, d), lambda i: (i, 0))
    acts, specs = [h2], [tile]
    if gated is not None:
        y2, gate2, w_o = gated
        acts += [y2, gate2, w_o.astype(_MXU)]
        specs += [tile, tile, _const_spec(w_o.shape)]
    return pl.pallas_call(
        _ffn_body if gated is None else _ffn_gated_body,
        out_shape=jax.ShapeDtypeStruct((rows, d), _F32),
        grid=(rows // tm,),
        in_specs=specs + [_const_spec((1, d)), _const_spec(win.shape), _const_spec(wout.shape)],
        out_specs=tile,
        compiler_params=_params("parallel"),
        name="ffn" if gated is None else "ffn_gated",
    )(*acts, g.reshape(1, d), win, wout)


def _ffn_tail(h, g, w_in, w_out, seq, heads=None):
    b, tp, d = h.shape
    tm = next((t for t in (512, SEQ_ALIGN) if seq % t == 0), None)
    if tm is None:
        if heads is not None:
            h = _dsa_out(h, *heads)
        return _ffn(h.reshape(b * tp, d), g, w_in, w_out).reshape(b, tp, d)[:, N_META:N_META + seq]
    win, wout = w_in.astype(_MXU), w_out.astype(_MXU)
    nxt = lambda bi, i: (bi, (i + 1) * (tm // N_META), 0)
    acts = [h, h]
    specs = [pl.BlockSpec((1, tm, d), lambda bi, i: (bi, i, 0)), pl.BlockSpec((1, N_META, d), nxt)]
    if heads is not None:
        y, w_o = heads
        acts += [y, y, w_o.astype(_MXU)]
        specs += [pl.BlockSpec((1, ATT_HEADS, tm, HEAD_DIM), lambda bi, i: (bi, 0, i, 0)),
                  pl.BlockSpec((1, ATT_HEADS, N_META, HEAD_DIM), lambda bi, i: (bi, 0, (i + 1) * (tm // N_META), 0)),
                  _const_spec(w_o.shape)]
    return pl.pallas_call(
        _ffn_tail_body if heads is None else _ffn_tail_heads_body,
        out_shape=jax.ShapeDtypeStruct((b, seq, d), _F32),
        grid=(b, seq // tm),
        in_specs=specs + [_const_spec((1, d)), _const_spec(win.shape), _const_spec(wout.shape)],
        out_specs=pl.BlockSpec((1, tm, d), lambda bi, i: (bi, i, 0)),
        compiler_params=_params("parallel", "parallel"),
        name="ffn_tail",
    )(*acts, g.reshape(1, d), win, wout)


def _rwkv_prep_body(h_ref, hprev_ref, ng_ref, mix_ref, wr_ref, wk_ref, wv_ref, w1_ref, w2_ref,
                    a1_ref, a2_ref, g1_ref, g2_ref, w0_ref, a0_ref,
                    r_ref, k_ref, v_ref, lw_ref, a_ref, g_ref):
    ng = ng_ref[...]
    hn = _rms(h_ref[0]) * ng
    prev = (_rms(hprev_ref[0]) * ng)[7:8, :]
    prev = jnp.where(pl.program_id(1) == 0, 0.0, prev)
    row = lax.broadcasted_iota(jnp.int32, hn.shape, 0)
    shifted = jnp.where(row == 0, prev, pltpu.roll(hn, 1, 0))
    xx = shifted - hn

    def mixed(j):
        return (hn + xx * mix_ref[j:j + 1, :]).astype(_MXU)

    r_ref[0] = _dot(mixed(0), wr_ref[...])
    k_ref[0] = _dot(mixed(2), wk_ref[...])
    v_ref[0] = _dot(mixed(3), wv_ref[...])
    wl = w0_ref[...] + _dot(jnp.tanh(_dot(mixed(1), w1_ref[...])), w2_ref[...])
    lw_ref[0] = -0.6065306597126334 * _sigmoid(wl)
    a_ref[0] = _sigmoid(a0_ref[...] + _dot(_dot(mixed(4), a1_ref[...]), a2_ref[...]))
    g_ref[0] = _dot(_sigmoid(_dot(mixed(5), g1_ref[...])), g2_ref[...])


def _rwkv_prep(h, ng, mix, w_r, w_k, w_v, w1, w2, a1, a2, g1, g2, w0, a0):
    b, tp, d = h.shape
    tm = SEQ_ALIGN
    ws = [w.astype(_MXU) for w in (w_r, w_k, w_v, w1, w2, a1, a2, g1, g2)]
    act = pl.BlockSpec((1, tm, d), lambda bi, i: (bi, i, 0))
    prev = pl.BlockSpec((1, 8, d), lambda bi, i: (bi, jnp.maximum(i * (tm // 8) - 1, 0), 0))
    out = jax.ShapeDtypeStruct((b, tp, d), _F32)
    return pl.pallas_call(
        _rwkv_prep_body,
        out_shape=[out] * 6,
        grid=(b, tp // tm),
        in_specs=[act, prev, _const_spec((1, d)), _const_spec(mix.shape)]
                 + [_const_spec(w.shape) for w in ws]
                 + [_const_spec((1, d)), _const_spec((1, d))],
        out_specs=[act] * 6,
        compiler_params=_params("parallel", "parallel"),
        name="rwkv_prep",
    )(h, h, ng.reshape(1, d), mix, *ws, w0.reshape(1, d), a0.reshape(1, d))


def _wkv_body(r_ref, k_ref, v_ref, lw_ref, a_ref, kk_ref, ka_ref, rk_ref, lg_ref, lb_ref,
              o_ref, s_ref, *, n_chunks, n_pairs):
    c_len = WKV_CHUNK

    @pl.when(pl.program_id(2) == 0)
    def _():
        s_ref[...] = jnp.zeros(s_ref.shape, _F32)

    lane = lax.broadcasted_iota(jnp.int32, (c_len, LANES), 1)
    row = lax.broadcasted_iota(jnp.int32, (c_len, LANES), 0)
    first = lane < HEAD_DIM
    col = jnp.where(first, lane, lane - HEAD_DIM)
    strict = col < row
    incl = col <= row
    sq_r = lax.broadcasted_iota(jnp.int32, (LANES, LANES), 0)
    sq_c = lax.broadcasted_iota(jnp.int32, (LANES, LANES), 1)
    same_head = (sq_r < HEAD_DIM) == (sq_c < HEAD_DIM)
    eye = sq_r == sq_c
    tri_r = lax.broadcasted_iota(jnp.int32, (c_len, c_len), 0)
    tri_c = lax.broadcasted_iota(jnp.int32, (c_len, c_len), 1)
    tri = jnp.where(tri_c <= tri_r, 1.0, 0.0).astype(_MXU)

    def head_sum(x):
        s0 = jnp.sum(jnp.where(first, x, 0.0), axis=-1, keepdims=True)
        s1 = jnp.sum(jnp.where(first, 0.0, x), axis=-1, keepdims=True)
        return jnp.where(first, s0, s1)

    def blockdiag(x):
        f = first if x.shape[1] == LANES else jnp.concatenate([first] * (x.shape[1] // LANES), axis=1)
        xb = x.astype(_MXU)
        zero = jnp.zeros_like(xb)
        return jnp.concatenate([jnp.where(f, xb, zero), jnp.where(f, zero, xb)], axis=0)

    def apply(p, x):
        return jnp.dot(p.astype(_MXU), blockdiag(x), preferred_element_type=_F32)

    units = [(c, p) for c in range(n_chunks) for p in range(n_pairs)]
    st = {}
    for u in units:
        c, p = u
        sl = pl.ds(c * c_len, c_len)
        ln = slice(p * LANES, (p + 1) * LANES)
        r = r_ref[0, sl, ln]
        k_raw = k_ref[0, sl, ln]
        v = v_ref[0, sl, ln]
        lw = lw_ref[0, sl, ln]
        gate = a_ref[0, sl, ln]
        kk = k_raw * kk_ref[:, ln]
        kk = kk / jnp.maximum(jnp.sqrt(head_sum(kk * kk)), 1e-12)
        k = k_raw * (1.0 + (gate - 1.0) * ka_ref[:, ln])
        h3 = _split3(lw)
        cs = (jnp.dot(tri, h3[0], preferred_element_type=_F32)
              + jnp.dot(tri, h3[1], preferred_element_type=_F32)
              + jnp.dot(tri, h3[2], preferred_element_type=_F32))
        st[u] = dict(r=r, k=k, v=v, lw=lw, cs=cs, a_vec=-kk, b_vec=kk * gate)

    for u in units:
        d = st[u]
        cs, lw = d["cs"], d["lw"]
        cs_prev = cs - lw
        c_mid = cs[c_len // 2 - 1:c_len // 2, :]
        c_end = cs[c_len - 1:c_len, :]
        inv_rel = jnp.exp(c_mid - cs)
        to_end = jnp.exp(c_end - cs)
        d["a_abs"] = d["a_vec"] * jnp.exp(cs_prev)
        d["r_abs"] = d["r"] * jnp.exp(cs)
        d["b_end"] = d["b_vec"] * to_end
        d["k_end"] = d["k"] * to_end
        d["g_end"] = jnp.exp(c_end)
        lhs = jnp.concatenate([d["a_vec"] * jnp.exp(cs_prev - c_mid), d["r"] * jnp.exp(cs - c_mid)],
                              axis=0).astype(_MXU)
        d["sb"] = lax.dot_general(lhs, blockdiag(d["b_vec"] * inv_rel), (((1,), (1,)), ((), ())),
                                  preferred_element_type=_F32)
        d["sk"] = lax.dot_general(lhs, blockdiag(d["k"] * inv_rel), (((1,), (1,)), ((), ())),
                                  preferred_element_type=_F32)

    base_blk = (row // WKV_INV_BASE) == (col // WKV_INV_BASE)
    for u in units:
        d = st[u]
        d["l"] = jnp.where(strict, d["sb"][:c_len], 0.0)
        d["g_rb"] = jnp.where(incl, d["sb"][c_len:], 0.0)
        d["g_rk"] = jnp.where(incl, d["sk"][c_len:], 0.0)
        l_ak = jnp.where(strict, d["sk"][:c_len], 0.0)
        d["x0"] = jnp.concatenate([d["a_abs"], apply(l_ak, d["v"])], axis=1)
        l_base = jnp.where(base_blk, d["l"], 0.0)
        d["t"] = jnp.where(col == row, 1.0, l_base)
        d["q"] = apply(l_base, l_base)
    for u in units:
        d = st[u]
        d["t"] = d["t"] + apply(d["t"], d["q"])
    width = WKV_INV_BASE
    while width < c_len:
        off_blk = ((row // (2 * width)) == (col // (2 * width))) & ((row // width) != (col // width))
        for u in units:
            d = st[u]
            d["q"] = apply(jnp.where(off_blk, d["l"], 0.0), d["t"])
        for u in units:
            d = st[u]
            d["t"] = d["t"] + apply(d["t"], d["q"])
        width *= 2
    for u in units:
        d = st[u]
        d["x"] = apply(d["t"], d["x0"])

    for u in units:
        d = st[u]
        x = d["x"]
        w_a = x[:, :LANES]
        u0 = x[:, LANES:]
        gx = apply(d["g_rb"], x)
        d["q_eff"] = d["r_abs"] + gx[:, :LANES]
        d["y0"] = gx[:, LANES:] + apply(d["g_rk"], d["v"])
        d["m"] = jnp.where(same_head, _dot_tn(w_a, d["b_end"]), 0.0) + jnp.where(eye, d["g_end"], 0.0)
        d["n"] = jnp.where(same_head, _dot_tn(u0, d["b_end"]) + _dot_tn(d["v"], d["k_end"]), 0.0)

    state = [s_ref[p] for p in range(n_pairs)]
    for u in units:
        c, p = u
        d = st[u]
        s_hi, s_lo = _split2(state[p])
        d["y"] = d["y0"] + _dot_nt(d["q_eff"], s_hi) + _dot_nt(d["q_eff"], s_lo)
        m_hi, m_lo = _split2(d["m"])
        state[p] = (jnp.dot(s_hi, m_hi, preferred_element_type=_F32)
                    + jnp.dot(s_hi, m_lo, preferred_element_type=_F32)
                    + jnp.dot(s_lo, m_hi, preferred_element_type=_F32) + d["n"])
    for p in range(n_pairs):
        s_ref[p] = state[p]

    for u in units:
        c, p = u
        d = st[u]
        ln = slice(p * LANES, (p + 1) * LANES)
        y = d["y"]
        mu = head_sum(y) * (1.0 / HEAD_DIM)
        yc = y - mu
        var = head_sum(yc * yc) * (1.0 / HEAD_DIM)
        yn = yc * lax.rsqrt(var + RWKV_LN_EPS) * lg_ref[:, ln] + lb_ref[:, ln]
        o_ref[0, pl.ds(c * c_len, c_len), ln] = yn + head_sum(d["r"] * d["k"] * rk_ref[:, ln]) * d["v"]


def _wkv(r, k, v, lw, a, k_k, k_a, r_k, ln_g, ln_b):
    b, tp, d = r.shape
    tc = WKV_CHUNK * WKV_CHUNKS_PER_STEP
    width = LANES * WKV_PAIRS_PER_STEP
    act = pl.BlockSpec((1, tc, width), lambda bi, p, s: (bi, s, p))
    par = pl.BlockSpec((1, width), lambda bi, p, s: (0, p))
    return pl.pallas_call(
        functools.partial(_wkv_body, n_chunks=WKV_CHUNKS_PER_STEP, n_pairs=WKV_PAIRS_PER_STEP),
        out_shape=jax.ShapeDtypeStruct((b, tp, d), _F32),
        grid=(b, d // width, tp // tc),
        in_specs=[act] * 5 + [par] * 5,
        out_specs=act,
        scratch_shapes=[pltpu.VMEM((WKV_PAIRS_PER_STEP, LANES, LANES), _F32)],
        compiler_params=_params("parallel", "parallel", "arbitrary"),
        name="wkv",
    )(r, k, v, lw, a, *(p.reshape(1, d) for p in (k_k, k_a, r_k, ln_g, ln_b)))


def _gated_out_body(h_ref, y_ref, g_ref, w_ref, o_ref):
    o_ref[...] = h_ref[...] + _dot(y_ref[...] * g_ref[...], w_ref[...])


def _gated_out(h2, y2, gate2, w):
    rows, d = h2.shape
    tm = 512 if rows % 512 == 0 else SEQ_ALIGN
    tile = pl.BlockSpec((tm, d), lambda i: (i, 0))
    return pl.pallas_call(
        _gated_out_body,
        out_shape=jax.ShapeDtypeStruct((rows, d), _F32),
        grid=(rows // tm,),
        in_specs=[tile] * 3 + [_const_spec(w.shape)],
        out_specs=tile,
        compiler_params=_params("parallel"),
        name="rwkv_out",
    )(h2, y2, gate2, w.astype(_MXU))


def _rwkv_mix(h, ng, p):
    r, k, v, lw, a, g = _rwkv_prep(h, ng, p["mix"], p["w_r"], p["w_k"], p["w_v"], p["w1"], p["w2"],
                                   p["a1"], p["a2"], p["g1"], p["g2"], p["w0"], p["a0"])
    return _wkv(r, k, v, lw, a, p["k_k"], p["k_a"], p["r_k"], p["lnx_g"], p["lnx_b"]), g


def _rwkv_layer(h, ng, p):
    b, tp, d = h.shape
    y, g = _rwkv_mix(h, ng, p)
    flat = lambda t: t.reshape(b * tp, d)
    return _gated_out(flat(h), flat(y), flat(g), p["w_o"]).reshape(b, tp, d)


def _rope_tables(tp):
    half = ROPE_DIM // 2
    inv = ROPE_THETA ** (-jnp.arange(0, ROPE_DIM, 2, dtype=_F32) / ROPE_DIM)
    ang = jnp.arange(tp, dtype=_F32)[:, None] * inv[None, :]
    cos, sin = jnp.cos(ang), jnp.sin(ang)
    ones = jnp.ones((tp, HEAD_DIM - ROPE_DIM), _F32)
    zeros = jnp.zeros((tp, HEAD_DIM - ROPE_DIM), _F32)
    zh = jnp.zeros((tp, half), _F32)
    c = jnp.concatenate([cos, cos, ones], axis=1)
    s_up = jnp.concatenate([-sin, zh, zeros], axis=1)
    s_dn = jnp.concatenate([zh, sin, zeros], axis=1)
    return tuple(jnp.concatenate([t, t], axis=1) for t in (c, s_up, s_dn))


def _dsa_proj_body(h_ref, ng_ref, wq_ref, wk_ref, wv_ref, wqi_ref, ws_ref, qg_ref, kg_ref,
                   c_ref, su_ref, sd_ref, q_ref, k_ref, v_ref, qi_ref, ki_ref, wi_ref):
    hb = (_rms(h_ref[0]) * ng_ref[...]).astype(_MXU)
    c, s_up, s_dn = c_ref[...], su_ref[...], sd_ref[...]
    lane = lax.broadcasted_iota(jnp.int32, c.shape, 1)
    first = lane < HEAD_DIM

    def rope(x):
        return x * c + pltpu.roll(x, LANES - ROPE_DIM // 2, 1) * s_up + pltpu.roll(x, ROPE_DIM // 2, 1) * s_dn

    sq_r = lax.broadcasted_iota(jnp.int32, (LANES, LANES), 0)
    sq_c = lax.broadcasted_iota(jnp.int32, (LANES, LANES), 1)
    head_ones = jnp.where((sq_r < HEAD_DIM) == (sq_c < HEAD_DIM), 1.0, 0.0).astype(_MXU)

    def head_rms(x, gain):
        hi, lo = _split2(x * x)
        ss = (jnp.dot(hi, head_ones, preferred_element_type=_F32)
              + jnp.dot(lo, head_ones, preferred_element_type=_F32))
        return x * lax.rsqrt(ss * (1.0 / HEAD_DIM) + NORM_EPS) * gain

    def put_heads(dst_ref, pair, x):
        dst_ref[0, 2 * pair] = x[:, :HEAD_DIM].astype(dst_ref.dtype)
        dst_ref[0, 2 * pair + 1] = x[:, HEAD_DIM:].astype(dst_ref.dtype)

    def with_ones(x):
        return jnp.where(first, x, jnp.where(lane == HEAD_DIM, 1.0, 0.0)).astype(v_ref.dtype)

    q = _dot(hb, wq_ref[...])
    for p in range(ATT_Q_W // LANES):
        blk = rope(head_rms(q[:, p * LANES:(p + 1) * LANES], qg_ref[...]))
        put_heads(q_ref, p, blk * (HEAD_DIM ** -0.5 * LOG2_E))
    k = _dot(hb, wk_ref[...])
    v = _dot(hb, wv_ref[...])
    for p in range(ATT_KV_W // LANES):
        put_heads(k_ref, p, rope(head_rms(k[:, p * LANES:(p + 1) * LANES], kg_ref[...])))
        vp = v[:, p * LANES:(p + 1) * LANES]
        v_ref[0, 2 * p] = with_ones(vp)
        v_ref[0, 2 * p + 1] = with_ones(pltpu.roll(vp, HEAD_DIM, 1))
    qi = _dot(hb, wqi_ref[...])
    for p in range(IDX_Q_W // LANES):
        put_heads(qi_ref, p, rope(qi[:, p * LANES:(p + 1) * LANES]))
    small = _dot(hb, ws_ref[...])
    ki_ref[0] = rope(small)[:, :IDX_DIM].astype(ki_ref.dtype)
    wi_ref[0] = small[:, IDX_DIM:IDX_DIM + IDX_HEADS] * (IDX_HEADS ** -0.5 * IDX_DIM ** -0.5)


def _dsa_proj(h, ng, w_in, q_g, k_g):
    b, tp, d = h.shape
    tm = SEQ_ALIGN
    o0, o1, o2, o3, o4 = (0, ATT_Q_W, ATT_Q_W + ATT_KV_W, ATT_Q_W + 2 * ATT_KV_W,
                          ATT_Q_W + 2 * ATT_KV_W + IDX_Q_W)
    w = w_in.astype(_MXU)
    w_small = jnp.pad(w[:, o4:], ((0, 0), (0, LANES - (w.shape[1] - o4))))
    ws = [w[:, o0:o1], w[:, o1:o2], w[:, o2:o3], w[:, o3:o4], w_small]
    tables = _rope_tables(tp)
    pair = lambda gvec: jnp.concatenate([gvec, gvec]).reshape(1, LANES)
    heads = lambda n: (jax.ShapeDtypeStruct((b, n, tp, HEAD_DIM), _MXU),
                       pl.BlockSpec((1, n, tm, HEAD_DIM), lambda bi, i: (bi, 0, i, 0)))
    outs = [heads(ATT_HEADS), heads(ATT_KV_HEADS),
            (jax.ShapeDtypeStruct((b, ATT_KV_HEADS, tp, LANES), _MXU),
             pl.BlockSpec((1, ATT_KV_HEADS, tm, LANES), lambda bi, i: (bi, 0, i, 0))),
            heads(IDX_HEADS),
            (jax.ShapeDtypeStruct((b, tp, IDX_DIM), _MXU), pl.BlockSpec((1, tm, IDX_DIM), lambda bi, i: (bi, i, 0))),
            (jax.ShapeDtypeStruct((b, tp, IDX_HEADS), _F32), pl.BlockSpec((1, tm, IDX_HEADS), lambda bi, i: (bi, i, 0)))]
    tab = pl.BlockSpec((tm, LANES), lambda bi, i: (i, 0))
    return pl.pallas_call(
        _dsa_proj_body,
        out_shape=[o[0] for o in outs],
        grid=(b, tp // tm),
        in_specs=[pl.BlockSpec((1, tm, d), lambda bi, i: (bi, i, 0)), _const_spec((1, d))]
                 + [_const_spec(x.shape) for x in ws]
                 + [_const_spec((1, LANES)), _const_spec((1, LANES)), tab, tab, tab],
        out_specs=[o[1] for o in outs],
        compiler_params=_params("parallel", "parallel"),
        name="dsa_proj",
    )(h, ng.reshape(1, d), *ws, pair(q_g), pair(k_g), *tables)


def _dsa_attn_body(qi_ref, wi_ref, ki_ref, q_ref, k_ref, v_ref, o_ref,
                   keys_ref, bias_ref, wb_ref, m_ref, acc_ref, thr_ref, cnt_ref, *, topk, widths):
    tq = SEQ_ALIGN
    tk = SEQ_ALIGN
    i = pl.program_id(1)
    n_tiles = i + 1
    n_key_tiles = keys_ref.shape[1] // tk

    keys_ref[...] = jnp.full(keys_ref.shape, INT_MIN, jnp.int32)
    wi = wi_ref[0]
    for hd in range(IDX_HEADS):
        wb_ref[hd] = jnp.broadcast_to(wi[:, hd:hd + 1], (tq, LANES))
    qi = qi_ref[0].reshape(IDX_HEADS * tq, IDX_DIM)
    row = lax.broadcasted_iota(jnp.int32, (tq, tk), 0)
    col = lax.broadcasted_iota(jnp.int32, (tq, tk), 1)

    def score_tiles(starts):
        zs = [_dot_nt(qi, ki_ref[0, pl.ds(start, tk), :]) for start in starts]
        for start, z in zip(starts, zs):
            score = jnp.zeros((tq, tk), _F32)
            for hd in range(IDX_HEADS):
                wb = wb_ref[hd]
                score = score + jnp.maximum(z[hd * tq:(hd + 1) * tq], 0.0) * jnp.concatenate([wb, wb], axis=1)
            bits = pltpu.bitcast(score, jnp.int32)
            key = bits ^ ((bits >> 31) & 0x7FFFFFFF)
            key = jnp.where(bits == INT_MIN, 0, key)
            key = jnp.maximum(key, INT_MIN + 1)
            causal = (start + col) <= (i * tq + row)
            keys_ref[:, pl.ds(start, tk)] = jnp.where(causal, key, INT_MIN)

    def score_pair(j, carry):
        start = pl.multiple_of(j * (2 * tk), 2 * tk)
        score_tiles([start, start + tk])
        return carry

    lax.fori_loop(0, n_tiles // 2, score_pair, 0)

    @pl.when(n_tiles % 2 == 1)
    def _():
        score_tiles([pl.multiple_of(i * tk, tk)])

    def count(mask):
        return jnp.sum(mask.astype(jnp.int32), axis=-1, keepdims=True)

    def select(width, cover):
        kv = keys_ref[:, :width]

        def value_bit(it, thr):
            cand = thr + (jnp.int32(1) << (31 - it))
            return jnp.where(count(kv >= cand) >= topk, cand, thr)

        thr = lax.fori_loop(0, 32, value_bit, jnp.full((tq, 1), INT_MIN, jnp.int32))
        floor = jnp.maximum(thr, INT_MIN + 1)
        picked = kv >= floor
        bias_ref[:, :width] = jnp.where(picked, 0.0, NEG_BIG)
        if cover > width:
            bias_ref[:, width:cover] = jnp.full((tq, cover - width), NEG_BIG, _F32)
        thr_ref[...] = thr
        cnt_ref[...] = count(picked)

    lo = 0
    for n_q, width, cover in widths:
        @pl.when((i >= lo) & (i < n_q))
        def _(width=width, cover=cover):
            select(width, cover)
        lo = n_q

    def break_ties(g, carry):
        rows8 = pl.ds(pl.multiple_of(g * 8, 8), 8)

        @pl.when(jnp.max(cnt_ref[rows8, :]) > topk)
        def _():
            kv = keys_ref[rows8, :]
            thr = thr_ref[rows8, :]
            above = kv > thr
            equal = (kv == thr) & (thr > INT_MIN)
            need = topk - count(above)
            pos = lax.broadcasted_iota(jnp.int32, kv.shape, 1)

            def pos_bit(it, last_short):
                cand = last_short + (jnp.int32(1) << (12 - it))
                return jnp.where(count(equal & (pos <= cand)) < need, cand, last_short)

            limit = lax.fori_loop(0, 13, pos_bit, jnp.full((8, 1), -1, jnp.int32)) + 1
            bias_ref[rows8, :] = jnp.where(above | (equal & (pos <= limit)), 0.0, NEG_BIG)
        return carry

    @pl.when(jnp.max(cnt_ref[...]) > topk)
    def _():
        lax.fori_loop(0, tq // 8, break_ties, 0)

    rows = ATT_GROUP * ATT_ROWS
    tiles_per_span = ATT_SPAN // tk
    span_tiles = (n_key_tiles // tiles_per_span) * tiles_per_span
    n_spans = (jnp.minimum(n_tiles, span_tiles) + tiles_per_span - 1) // tiles_per_span
    chains = [(gl, qh) for gl in range(ATT_GROUPS_PER_STEP) for qh in range(tq // ATT_ROWS)]

    def group_step(gs, carry):
        kv = [gs * ATT_GROUPS_PER_STEP + gl for gl, _ in chains]
        q_rows = [slice(qh * ATT_ROWS, (qh + 1) * ATT_ROWS) for _, qh in chains]
        qs = [q_ref[0, pl.ds(kv[c] * ATT_GROUP, ATT_GROUP), q_rows[c], :].reshape(rows, HEAD_DIM)
              for c in range(len(chains))]
        m_ref[...] = jnp.full(m_ref.shape, NEG_BIG, _F32)
        acc_ref[...] = jnp.zeros(acc_ref.shape, _F32)

        def flash(start, width):
            s = [_dot_nt(qs[c], k_ref[0, kv[c], pl.ds(start, width), :]) for c in range(len(chains))]
            p = []
            for c in range(len(chains)):
                bias = bias_ref[q_rows[c], pl.ds(start, width)]
                sc = (s[c].reshape(ATT_GROUP, ATT_ROWS, width) + bias[None]).reshape(rows, width)
                m_prev = m_ref[c]
                m_new = jnp.maximum(m_prev, jnp.max(sc, axis=-1, keepdims=True))
                acc_ref[c] = jnp.exp2(m_prev - m_new) * acc_ref[c]
                m_ref[c] = m_new
                p.append(jnp.exp2(sc - m_new).astype(_MXU))
            for c in range(len(chains)):
                acc_ref[c] += jnp.dot(p[c], v_ref[0, kv[c], pl.ds(start, width), :],
                                      preferred_element_type=_F32)

        def span_step(j, carry):
            flash(pl.multiple_of(j * ATT_SPAN, ATT_SPAN), ATT_SPAN)
            return carry

        lax.fori_loop(0, n_spans, span_step, 0)
        if n_key_tiles > span_tiles:
            @pl.when(n_tiles > span_tiles)
            def _():
                flash(span_tiles * tk, (n_key_tiles - span_tiles) * tk)
        for c in range(len(chains)):
            acc = acc_ref[c]
            out = acc[:, :HEAD_DIM] / acc[:, HEAD_DIM:HEAD_DIM + 1]
            o_ref[0, pl.ds(kv[c] * ATT_GROUP, ATT_GROUP), q_rows[c], :] = (
                out.reshape(ATT_GROUP, ATT_ROWS, HEAD_DIM).astype(o_ref.dtype))
        return carry

    lax.fori_loop(0, ATT_KV_HEADS // ATT_GROUPS_PER_STEP, group_step, 0)


def _dsa_attn(q, k, v, qi, ki, wi, topk):
    b, _, tp, _ = q.shape
    tq = SEQ_ALIGN
    nq = tp // tq
    tiles_per_span = ATT_SPAN // tq
    classes = sorted(set(range(SELECT_CLASS_TILES, nq + 1, SELECT_CLASS_TILES)) | {nq})
    widths = tuple((n, n * tq, min(-(-n // tiles_per_span) * tiles_per_span, nq) * tq) for n in classes)
    n_chains = ATT_GROUPS_PER_STEP * (tq // ATT_ROWS)
    per_q = lambda n: pl.BlockSpec((1, n, tq, HEAD_DIM), lambda bi, i: (bi, 0, i, 0))
    per_b = lambda n, w: pl.BlockSpec((1, n, tp, w), lambda bi, i: (bi, 0, 0, 0),
                                      pipeline_mode=pl.Buffered(1))
    return pl.pallas_call(
        functools.partial(_dsa_attn_body, topk=topk, widths=widths),
        out_shape=jax.ShapeDtypeStruct((b, ATT_HEADS, tp, HEAD_DIM), _MXU),
        grid=(b, nq),
        in_specs=[per_q(IDX_HEADS),
                  pl.BlockSpec((1, tq, IDX_HEADS), lambda bi, i: (bi, i, 0)),
                  pl.BlockSpec((1, tp, IDX_DIM), lambda bi, i: (bi, 0, 0), pipeline_mode=pl.Buffered(1)),
                  per_q(ATT_HEADS), per_b(ATT_KV_HEADS, HEAD_DIM), per_b(ATT_KV_HEADS, LANES)],
        out_specs=per_q(ATT_HEADS),
        scratch_shapes=[pltpu.VMEM((tq, tp), jnp.int32),
                        pltpu.VMEM((tq, tp), _F32),
                        pltpu.VMEM((IDX_HEADS, tq, LANES), _F32),
                        pltpu.VMEM((n_chains, ATT_GROUP * ATT_ROWS, 1), _F32),
                        pltpu.VMEM((n_chains, ATT_GROUP * ATT_ROWS, LANES), _F32),
                        pltpu.VMEM((tq, 1), jnp.int32),
                        pltpu.VMEM((tq, 1), jnp.int32)],
        compiler_params=_params("parallel", "arbitrary"),
        name="dsa_attn",
    )(qi, wi, ki, q, k, v)


def _dsa_out_body(h_ref, y_ref, w_ref, o_ref):
    y = jnp.concatenate([y_ref[0, hd] for hd in range(ATT_HEADS)], axis=1)
    o_ref[0] = h_ref[0] + _dot(y, w_ref[...])


def _dsa_out(h, y, w_o):
    b, tp, d = h.shape
    tm = SEQ_ALIGN
    tile = pl.BlockSpec((1, tm, d), lambda bi, i: (bi, i, 0))
    return pl.pallas_call(
        _dsa_out_body,
        out_shape=jax.ShapeDtypeStruct((b, tp, d), _F32),
        grid=(b, tp // tm),
        in_specs=[tile, pl.BlockSpec((1, ATT_HEADS, tm, HEAD_DIM), lambda bi, i: (bi, 0, i, 0)),
                  _const_spec(w_o.shape)],
        out_specs=tile,
        compiler_params=_params("parallel", "parallel"),
        name="dsa_out",
    )(h, y, w_o.astype(_MXU))


def _dsa_mix(h, ng, w_in, q_g, k_g, topk):
    q, k, v, qi, ki, wi = _dsa_proj(h, ng, w_in, q_g, k_g)
    return _dsa_attn(q, k, v, qi, ki, wi, topk)


def _dsa_layer(h, ng, w_in, q_g, k_g, w_o, topk):
    return _dsa_out(h, _dsa_mix(h, ng, w_in, q_g, k_g, topk), w_o)


def kernel(x, meta, norm_g, ffn_w_in, ffn_w_out, rk_mix, rk_w_r, rk_w_k, rk_w_v, rk_w_o, rk_w0, rk_w1,
           rk_w2, rk_a0, rk_a1, rk_a2, rk_g1, rk_g2, rk_k_k, rk_k_a, rk_r_k, rk_lnx_g, rk_lnx_b,
           at_w_in, at_q_g, at_k_g, at_w_o):
    b, seq, d = x.shape
    topk = min(TOPK_MAX, seq // 4)
    t = seq + N_META
    tp = -(-t // SEQ_ALIGN) * SEQ_ALIGN
    h = jnp.concatenate([jnp.broadcast_to(meta.astype(x.dtype)[None], (b, N_META, d)), x,
                         jnp.zeros((b, tp - t, d), x.dtype)], axis=1)
    rk = dict(mix=rk_mix, w_r=rk_w_r, w_k=rk_w_k, w_v=rk_w_v, w_o=rk_w_o, w0=rk_w0, w1=rk_w1, w2=rk_w2,
              a0=rk_a0, a1=rk_a1, a2=rk_a2, g1=rk_g1, g2=rk_g2, k_k=rk_k_k, k_a=rk_k_a, r_k=rk_r_k,
              lnx_g=rk_lnx_g, lnx_b=rk_lnx_b)
    n_mixers = 2
    depth = norm_g.shape[0]
    flat = lambda a: a.reshape(b * tp, d)
    for layer in range(depth):
        g = norm_g[layer]
        last = layer + 1 == depth
        w_in2, w_out2 = ffn_w_in[layer, 1], ffn_w_out[layer, 1]
        h = _ffn(flat(h), g[0], ffn_w_in[layer, 0], ffn_w_out[layer, 0]).reshape(b, tp, d)
        j = layer // n_mixers
        if layer % n_mixers == 0:
            p = {name: val[j] for name, val in rk.items()}
            if last:
                out = _ffn_tail(_rwkv_layer(h, g[1], p), g[2], w_in2, w_out2, seq)
            else:
                y, gate = _rwkv_mix(h, g[1], p)
                h = _ffn(flat(h), g[2], w_in2, w_out2, gated=(flat(y), flat(gate), p["w_o"])).reshape(b, tp, d)
        else:
            if last:
                y = _dsa_mix(h, g[1], at_w_in[j], at_q_g[j], at_k_g[j], topk)
                out = _ffn_tail(h, g[2], w_in2, w_out2, seq, heads=(y, at_w_o[j]))
            else:
                h = _dsa_layer(h, g[1], at_w_in[j], at_q_g[j], at_k_g[j], at_w_o[j], topk)
                h = _ffn(flat(h), g[2], w_in2, w_out2).reshape(b, tp, d)
    return out
```

```python
import functools

import jax
import jax.numpy as jnp
from jax import lax
from jax.experimental import pallas as pl
from jax.experimental.pallas import tpu as pltpu

D_MODEL = 1024
N_META = 16
D_FF = 2816
NORM_EPS = 1e-6
HEAD_DIM = 64
ROPE_DIM = HEAD_DIM // 4
ROPE_THETA = 500000.0
RWKV_HEADS = D_MODEL // HEAD_DIM
RWKV_LN_EPS = 64e-5
ATT_HEADS = D_MODEL // HEAD_DIM
ATT_KV_HEADS = 4
ATT_GROUP = ATT_HEADS // ATT_KV_HEADS
IDX_HEADS = 8
IDX_DIM = 64
TOPK_MAX = 256
ATT_Q_W = ATT_HEADS * HEAD_DIM
ATT_KV_W = ATT_KV_HEADS * HEAD_DIM
IDX_Q_W = IDX_HEADS * IDX_DIM

LANES = 128
SEQ_ALIGN = 256
VMEM_LIMIT = 56 * 1024 * 1024
FF_CHUNK = 256
WKV_CHUNK = 64
WKV_CHUNKS_PER_STEP = 4
WKV_INV_BASE = 4
WKV_PAIRS_PER_STEP = 8
ATT_ROWS = 128
ATT_SPAN = 1024
ATT_GROUPS_PER_STEP = 2
SELECT_CLASS_TILES = 1
INT_MIN = -(2 ** 31)
NEG_BIG = -1e30
LOG2_E = 1.4426950408889634

_MXU = jnp.bfloat16
_F32 = jnp.float32


def _dot(a, b):
    return jnp.dot(a.astype(_MXU), b.astype(_MXU), preferred_element_type=_F32)


def _dot_nt(a, b):
    return lax.dot_general(a.astype(_MXU), b.astype(_MXU), (((1,), (1,)), ((), ())),
                           preferred_element_type=_F32)


def _dot_tn(a, b):
    return lax.dot_general(a.astype(_MXU), b.astype(_MXU), (((0,), (0,)), ((), ())),
                           preferred_element_type=_F32)


def _split2(x):
    hi = x.astype(_MXU)
    lo = (x - hi.astype(_F32)).astype(_MXU)
    return hi, lo


def _split3(x):
    hi = x.astype(_MXU)
    r1 = x - hi.astype(_F32)
    mid = r1.astype(_MXU)
    lo = (r1 - mid.astype(_F32)).astype(_MXU)
    return hi, mid, lo


def _rms(x, eps=NORM_EPS):
    return x * lax.rsqrt(jnp.mean(x * x, axis=-1, keepdims=True) + eps)


def _sigmoid(x):
    return 1.0 / (1.0 + jnp.exp(-x))


def _params(*sem):
    return pltpu.CompilerParams(dimension_semantics=sem, vmem_limit_bytes=VMEM_LIMIT)


def _const_spec(shape):
    n = len(shape)
    return pl.BlockSpec(shape, lambda *_: (0,) * n, pipeline_mode=pl.Buffered(1))


def _swiglu_residual(x, g_ref, win_ref, wout_ref):
    xb = (_rms(x) * g_ref[...]).astype(_MXU)
    acc = jnp.zeros(x.shape, _F32)
    for c in range(D_FF // FF_CHUNK):
        lo = c * FF_CHUNK
        gate = _dot(xb, win_ref[:, lo:lo + FF_CHUNK])
        up = _dot(xb, win_ref[:, D_FF + lo:D_FF + lo + FF_CHUNK])
        act = gate * _sigmoid(gate) * up
        acc = acc + _dot(act, wout_ref[lo:lo + FF_CHUNK, :])
    return x + 0.5 * acc


def _ffn_body(h_ref, g_ref, win_ref, wout_ref, o_ref):
    o_ref[...] = _swiglu_residual(h_ref[...], g_ref, win_ref, wout_ref)


def _ffn_gated_body(h_ref, y_ref, gate_ref, wo_ref, g_ref, win_ref, wout_ref, o_ref):
    x = h_ref[...] + _dot(y_ref[...] * gate_ref[...], wo_ref[...])
    o_ref[...] = _swiglu_residual(x, g_ref, win_ref, wout_ref)


def _shifted(ref, next_ref):
    return jnp.concatenate([ref[N_META:, :], next_ref[...]], axis=0)


def _ffn_tail_body(h_ref, hn_ref, g_ref, win_ref, wout_ref, o_ref):
    x = _shifted(h_ref.at[0], hn_ref.at[0])
    o_ref[0] = _swiglu_residual(x, g_ref, win_ref, wout_ref)


def _ffn_tail_heads_body(h_ref, hn_ref, y_ref, yn_ref, wo_ref, g_ref, win_ref, wout_ref, o_ref):
    y = jnp.concatenate([_shifted(y_ref.at[0, hd], yn_ref.at[0, hd]) for hd in range(ATT_HEADS)], axis=1)
    x = _shifted(h_ref.at[0], hn_ref.at[0]) + _dot(y, wo_ref[...])
    o_ref[0] = _swiglu_residual(x, g_ref, win_ref, wout_ref)


def _ffn(h2, g, w_in, w_out, gated=None):
    rows, d = h2.shape
    tm = 512 if rows % 512 == 0 else SEQ_ALIGN
    win, wout = w_in.astype(_MXU), w_out.astype(_MXU)
    tile = pl.BlockSpec((tm, d), lambda i: (i, 0))
    acts, specs = [h2], [tile]
    if gated is not None:
        y2, gate2, w_o = gated
        acts += [y2, gate2, w_o.astype(_MXU)]
        specs += [tile, tile, _const_spec(w_o.shape)]
    return pl.pallas_call(
        _ffn_body if gated is None else _ffn_gated_body,
        out_shape=jax.ShapeDtypeStruct((rows, d), _F32),
        grid=(rows // tm,),
        in_specs=specs + [_const_spec((1, d)), _const_spec(win.shape), _const_spec(wout.shape)],
        out_specs=tile,
        compiler_params=_params("parallel"),
        name="ffn" if gated is None else "ffn_gated",
    )(*acts, g.reshape(1, d), win, wout)


def _ffn_tail(h, g, w_in, w_out, seq, heads=None):
    b, tp, d = h.shape
    tm = next((t for t in (512, SEQ_ALIGN) if seq % t == 0), None)
    if tm is None:
        if heads is not None:
            h = _dsa_out(h, *heads)
        return _ffn(h.reshape(b * tp, d), g, w_in, w_out).reshape(b, tp, d)[:, N_META:N_META + seq]
    win, wout = w_in.astype(_MXU), w_out.astype(_MXU)
    nxt = lambda bi, i: (bi, (i + 1) * (tm // N_META), 0)
    acts = [h, h]
    specs = [pl.BlockSpec((1, tm, d), lambda bi, i: (bi, i, 0)), pl.BlockSpec((1, N_META, d), nxt)]
    if heads is not None:
        y, w_o = heads
        acts += [y, y, w_o.astype(_MXU)]
        specs += [pl.BlockSpec((1, ATT_HEADS, tm, HEAD_DIM), lambda bi, i: (bi, 0, i, 0)),
                  pl.BlockSpec((1, ATT_HEADS, N_META, HEAD_DIM), lambda bi, i: (bi, 0, (i + 1) * (tm // N_META), 0)),
                  _const_spec(w_o.shape)]
    return pl.pallas_call(
        _ffn_tail_body if heads is None else _ffn_tail_heads_body,
        out_shape=jax.ShapeDtypeStruct((b, seq, d), _F32),
        grid=(b, seq // tm),
        in_specs=specs + [_const_spec((1, d)), _const_spec(win.shape), _const_spec(wout.shape)],
        out_specs=pl.BlockSpec((1, tm, d), lambda bi, i: (bi, i, 0)),
        compiler_params=_params("parallel", "parallel"),
        name="ffn_tail",
    )(*acts, g.reshape(1, d), win, wout)


def _rwkv_prep_body(h_ref, hprev_ref, ng_ref, mix_ref, wr_ref, wk_ref, wv_ref, w1_ref, w2_ref,
                    a1_ref, a2_ref, g1_ref, g2_ref, w0_ref, a0_ref,
                    r_ref, k_ref, v_ref, lw_ref, a_ref, g_ref):
    ng = ng_ref[...]
    hn = _rms(h_ref[0]) * ng
    prev = (_rms(hprev_ref[0]) * ng)[7:8, :]
    prev = jnp.where(pl.program_id(1) == 0, 0.0, prev)
    row = lax.broadcasted_iota(jnp.int32, hn.shape, 0)
    shifted = jnp.where(row == 0, prev, pltpu.roll(hn, 1, 0))
    xx = shifted - hn

    def mixed(j):
        return (hn + xx * mix_ref[j:j + 1, :]).astype(_MXU)

    r_ref[0] = _dot(mixed(0), wr_ref[...])
    k_ref[0] = _dot(mixed(2), wk_ref[...])
    v_ref[0] = _dot(mixed(3), wv_ref[...])
    wl = w0_ref[...] + _dot(jnp.tanh(_dot(mixed(1), w1_ref[...])), w2_ref[...])
    lw_ref[0] = -0.6065306597126334 * _sigmoid(wl)
    a_ref[0] = _sigmoid(a0_ref[...] + _dot(_dot(mixed(4), a1_ref[...]), a2_ref[...]))
    g_ref[0] = _dot(_sigmoid(_dot(mixed(5), g1_ref[...])), g2_ref[...])


def _rwkv_prep(h, ng, mix, w_r, w_k, w_v, w1, w2, a1, a2, g1, g2, w0, a0):
    b, tp, d = h.shape
    tm = SEQ_ALIGN
    ws = [w.astype(_MXU) for w in (w_r, w_k, w_v, w1, w2, a1, a2, g1, g2)]
    act = pl.BlockSpec((1, tm, d), lambda bi, i: (bi, i, 0))
    prev = pl.BlockSpec((1, 8, d), lambda bi, i: (bi, jnp.maximum(i * (tm // 8) - 1, 0), 0))
    out = jax.ShapeDtypeStruct((b, tp, d), _F32)
    return pl.pallas_call(
        _rwkv_prep_body,
        out_shape=[out] * 6,
        grid=(b, tp // tm),
        in_specs=[act, prev, _const_spec((1, d)), _const_spec(mix.shape)]
                 + [_const_spec(w.shape) for w in ws]
                 + [_const_spec((1, d)), _const_spec((1, d))],
        out_specs=[act] * 6,
        compiler_params=_params("parallel", "parallel"),
        name="rwkv_prep",
    )(h, h, ng.reshape(1, d), mix, *ws, w0.reshape(1, d), a0.reshape(1, d))


def _wkv_body(r_ref, k_ref, v_ref, lw_ref, a_ref, kk_ref, ka_ref, rk_ref, lg_ref, lb_ref,
              o_ref, s_ref, *, n_chunks, n_pairs):
    c_len = WKV_CHUNK

    @pl.when(pl.program_id(2) == 0)
    def _():
        s_ref[...] = jnp.zeros(s_ref.shape, _F32)

    lane = lax.broadcasted_iota(jnp.int32, (c_len, LANES), 1)
    row = lax.broadcasted_iota(jnp.int32, (c_len, LANES), 0)
    first = lane < HEAD_DIM
    col = jnp.where(first, lane, lane - HEAD_DIM)
    strict = col < row
    incl = col <= row
    sq_r = lax.broadcasted_iota(jnp.int32, (LANES, LANES), 0)
    sq_c = lax.broadcasted_iota(jnp.int32, (LANES, LANES), 1)
    same_head = (sq_r < HEAD_DIM) == (sq_c < HEAD_DIM)
    eye = sq_r == sq_c
    tri_r = lax.broadcasted_iota(jnp.int32, (c_len, c_len), 0)
    tri_c = lax.broadcasted_iota(jnp.int32, (c_len, c_len), 1)
    tri = jnp.where(tri_c <= tri_r, 1.0, 0.0).astype(_MXU)

    def head_sum(x):
        s0 = jnp.sum(jnp.where(first, x, 0.0), axis=-1, keepdims=True)
        s1 = jnp.sum(jnp.where(first, 0.0, x), axis=-1, keepdims=True)
        return jnp.where(first, s0, s1)

    def blockdiag(x):
        f = first if x.shape[1] == LANES else jnp.concatenate([first] * (x.shape[1] // LANES), axis=1)
        xb = x.astype(_MXU)
        zero = jnp.zeros_like(xb)
        return jnp.concatenate([jnp.where(f, xb, zero), jnp.where(f, zero, xb)], axis=0)

    def apply(p, x):
        return jnp.dot(p.astype(_MXU), blockdiag(x), preferred_element_type=_F32)

    units = [(c, p) for c in range(n_chunks) for p in range(n_pairs)]
    st = {}
    for u in units:
        c, p = u
        sl = pl.ds(c * c_len, c_len)
        ln = slice(p * LANES, (p + 1) * LANES)
        r = r_ref[0, sl, ln]
        k_raw = k_ref[0, sl, ln]
        v = v_ref[0, sl, ln]
        lw = lw_ref[0, sl, ln]
        gate = a_ref[0, sl, ln]
        kk = k_raw * kk_ref[:, ln]
        kk = kk / jnp.maximum(jnp.sqrt(head_sum(kk * kk)), 1e-12)
        k = k_raw * (1.0 + (gate - 1.0) * ka_ref[:, ln])
        h3 = _split3(lw)
        cs = (jnp.dot(tri, h3[0], preferred_element_type=_F32)
              + jnp.dot(tri, h3[1], preferred_element_type=_F32)
              + jnp.dot(tri, h3[2], preferred_element_type=_F32))
        st[u] = dict(r=r, k=k, v=v, lw=lw, cs=cs, a_vec=-kk, b_vec=kk * gate)

    for u in units:
        d = st[u]
        cs, lw = d["cs"], d["lw"]
        cs_prev = cs - lw
        c_mid = cs[c_len // 2 - 1:c_len // 2, :]
        c_end = cs[c_len - 1:c_len, :]
        inv_rel = jnp.exp(c_mid - cs)
        to_end = jnp.exp(c_end - cs)
        d["a_abs"] = d["a_vec"] * jnp.exp(cs_prev)
        d["r_abs"] = d["r"] * jnp.exp(cs)
        d["b_end"] = d["b_vec"] * to_end
        d["k_end"] = d["k"] * to_end
        d["g_end"] = jnp.exp(c_end)
        lhs = jnp.concatenate([d["a_vec"] * jnp.exp(cs_prev - c_mid), d["r"] * jnp.exp(cs - c_mid)],
                              axis=0).astype(_MXU)
        d["sb"] = lax.dot_general(lhs, blockdiag(d["b_vec"] * inv_rel), (((1,), (1,)), ((), ())),
                                  preferred_element_type=_F32)
        d["sk"] = lax.dot_general(lhs, blockdiag(d["k"] * inv_rel), (((1,), (1,)), ((), ())),
                                  preferred_element_type=_F32)

    base_blk = (row // WKV_INV_BASE) == (col // WKV_INV_BASE)
    for u in units:
        d = st[u]
        d["l"] = jnp.where(strict, d["sb"][:c_len], 0.0)
        d["g_rb"] = jnp.where(incl, d["sb"][c_len:], 0.0)
        d["g_rk"] = jnp.where(incl, d["sk"][c_len:], 0.0)
        l_ak = jnp.where(strict, d["sk"][:c_len], 0.0)
        d["x0"] = jnp.concatenate([d["a_abs"], apply(l_ak, d["v"])], axis=1)
        l_base = jnp.where(base_blk, d["l"], 0.0)
        d["t"] = jnp.where(col == row, 1.0, l_base)
        d["q"] = apply(l_base, l_base)
    for u in units:
        d = st[u]
        d["t"] = d["t"] + apply(d["t"], d["q"])
    width = WKV_INV_BASE
    while width < c_len:
        off_blk = ((row // (2 * width)) == (col // (2 * width))) & ((row // width) != (col // width))
        for u in units:
            d = st[u]
            d["q"] = apply(jnp.where(off_blk, d["l"], 0.0), d["t"])
        for u in units:
            d = st[u]
            d["t"] = d["t"] + apply(d["t"], d["q"])
        width *= 2
    for u in units:
        d = st[u]
        d["x"] = apply(d["t"], d["x0"])

    for u in units:
        d = st[u]
        x = d["x"]
        w_a = x[:, :LANES]
        u0 = x[:, LANES:]
        gx = apply(d["g_rb"], x)
        d["q_eff"] = d["r_abs"] + gx[:, :LANES]
        d["y0"] = gx[:, LANES:] + apply(d["g_rk"], d["v"])
        d["m"] = jnp.where(same_head, _dot_tn(w_a, d["b_end"]), 0.0) + jnp.where(eye, d["g_end"], 0.0)
        d["n"] = jnp.where(same_head, _dot_tn(u0, d["b_end"]) + _dot_tn(d["v"], d["k_end"]), 0.0)

    state = [s_ref[p] for p in range(n_pairs)]
    for u in units:
        c, p = u
        d = st[u]
        s_hi, s_lo = _split2(state[p])
        d["y"] = d["y0"] + _dot_nt(d["q_eff"], s_hi) + _dot_nt(d["q_eff"], s_lo)
        m_hi, m_lo = _split2(d["m"])
        state[p] = (jnp.dot(s_hi, m_hi, preferred_element_type=_F32)
                    + jnp.dot(s_hi, m_lo, preferred_element_type=_F32)
                    + jnp.dot(s_lo, m_hi, preferred_element_type=_F32) + d["n"])
    for p in range(n_pairs):
        s_ref[p] = state[p]

    for u in units:
        c, p = u
        d = st[u]
        ln = slice(p * LANES, (p + 1) * LANES)
        y = d["y"]
        mu = head_sum(y) * (1.0 / HEAD_DIM)
        yc = y - mu
        var = head_sum(yc * yc) * (1.0 / HEAD_DIM)
        yn = yc * lax.rsqrt(var + RWKV_LN_EPS) * lg_ref[:, ln] + lb_ref[:, ln]
        o_ref[0, pl.ds(c * c_len, c_len), ln] = yn + head_sum(d["r"] * d["k"] * rk_ref[:, ln]) * d["v"]


def _wkv(r, k, v, lw, a, k_k, k_a, r_k, ln_g, ln_b):
    b, tp, d = r.shape
    tc = WKV_CHUNK * WKV_CHUNKS_PER_STEP
    width = LANES * WKV_PAIRS_PER_STEP
    act = pl.BlockSpec((1, tc, width), lambda bi, p, s: (bi, s, p))
    par = pl.BlockSpec((1, width), lambda bi, p, s: (0, p))
    return pl.pallas_call(
        functools.partial(_wkv_body, n_chunks=WKV_CHUNKS_PER_STEP, n_pairs=WKV_PAIRS_PER_STEP),
        out_shape=jax.ShapeDtypeStruct((b, tp, d), _F32),
        grid=(b, d // width, tp // tc),
        in_specs=[act] * 5 + [par] * 5,
        out_specs=act,
        scratch_shapes=[pltpu.VMEM((WKV_PAIRS_PER_STEP, LANES, LANES), _F32)],
        compiler_params=_params("parallel", "parallel", "arbitrary"),
        name="wkv",
    )(r, k, v, lw, a, *(p.reshape(1, d) for p in (k_k, k_a, r_k, ln_g, ln_b)))


def _gated_out_body(h_ref, y_ref, g_ref, w_ref, o_ref):
    o_ref[...] = h_ref[...] + _dot(y_ref[...] * g_ref[...], w_ref[...])


def _gated_out(h2, y2, gate2, w):
    rows, d = h2.shape
    tm = 512 if rows % 512 == 0 else SEQ_ALIGN
    tile = pl.BlockSpec((tm, d), lambda i: (i, 0))
    return pl.pallas_call(
        _gated_out_body,
        out_shape=jax.ShapeDtypeStruct((rows, d), _F32),
        grid=(rows // tm,),
        in_specs=[tile] * 3 + [_const_spec(w.shape)],
        out_specs=tile,
        compiler_params=_params("parallel"),
        name="rwkv_out",
    )(h2, y2, gate2, w.astype(_MXU))


def _rwkv_mix(h, ng, p):
    r, k, v, lw, a, g = _rwkv_prep(h, ng, p["mix"], p["w_r"], p["w_k"], p["w_v"], p["w1"], p["w2"],
                                   p["a1"], p["a2"], p["g1"], p["g2"], p["w0"], p["a0"])
    return _wkv(r, k, v, lw, a, p["k_k"], p["k_a"], p["r_k"], p["lnx_g"], p["lnx_b"]), g


def _rwkv_layer(h, ng, p):
    b, tp, d = h.shape
    y, g = _rwkv_mix(h, ng, p)
    flat = lambda t: t.reshape(b * tp, d)
    return _gated_out(flat(h), flat(y), flat(g), p["w_o"]).reshape(b, tp, d)


def _rope_tables(tp):
    half = ROPE_DIM // 2
    inv = ROPE_THETA ** (-jnp.arange(0, ROPE_DIM, 2, dtype=_F32) / ROPE_DIM)
    ang = jnp.arange(tp, dtype=_F32)[:, None] * inv[None, :]
    cos, sin = jnp.cos(ang), jnp.sin(ang)
    ones = jnp.ones((tp, HEAD_DIM - ROPE_DIM), _F32)
    zeros = jnp.zeros((tp, HEAD_DIM - ROPE_DIM), _F32)
    zh = jnp.zeros((tp, half), _F32)
    c = jnp.concatenate([cos, cos, ones], axis=1)
    s_up = jnp.concatenate([-sin, zh, zeros], axis=1)
    s_dn = jnp.concatenate([zh, sin, zeros], axis=1)
    return tuple(jnp.concatenate([t, t], axis=1) for t in (c, s_up, s_dn))


def _dsa_proj_body(h_ref, ng_ref, wq_ref, wk_ref, wv_ref, wqi_ref, ws_ref, qg_ref, kg_ref,
                   c_ref, su_ref, sd_ref, q_ref, k_ref, v_ref, qi_ref, ki_ref, wi_ref):
    hb = (_rms(h_ref[0]) * ng_ref[...]).astype(_MXU)
    c, s_up, s_dn = c_ref[...], su_ref[...], sd_ref[...]
    lane = lax.broadcasted_iota(jnp.int32, c.shape, 1)
    first = lane < HEAD_DIM

    def rope(x):
        return x * c + pltpu.roll(x, LANES - ROPE_DIM // 2, 1) * s_up + pltpu.roll(x, ROPE_DIM // 2, 1) * s_dn

    sq_r = lax.broadcasted_iota(jnp.int32, (LANES, LANES), 0)
    sq_c = lax.broadcasted_iota(jnp.int32, (LANES, LANES), 1)
    head_ones = jnp.where((sq_r < HEAD_DIM) == (sq_c < HEAD_DIM), 1.0, 0.0).astype(_MXU)

    def head_rms(x, gain):
        hi, lo = _split2(x * x)
        ss = (jnp.dot(hi, head_ones, preferred_element_type=_F32)
              + jnp.dot(lo, head_ones, preferred_element_type=_F32))
        return x * lax.rsqrt(ss * (1.0 / HEAD_DIM) + NORM_EPS) * gain

    def put_heads(dst_ref, pair, x):
        dst_ref[0, 2 * pair] = x[:, :HEAD_DIM].astype(dst_ref.dtype)
        dst_ref[0, 2 * pair + 1] = x[:, HEAD_DIM:].astype(dst_ref.dtype)

    def with_ones(x):
        return jnp.where(first, x, jnp.where(lane == HEAD_DIM, 1.0, 0.0)).astype(v_ref.dtype)

    q = _dot(hb, wq_ref[...])
    for p in range(ATT_Q_W // LANES):
        blk = rope(head_rms(q[:, p * LANES:(p + 1) * LANES], qg_ref[...]))
        put_heads(q_ref, p, blk * (HEAD_DIM ** -0.5 * LOG2_E))
    k = _dot(hb, wk_ref[...])
    v = _dot(hb, wv_ref[...])
    for p in range(ATT_KV_W // LANES):
        put_heads(k_ref, p, rope(head_rms(k[:, p * LANES:(p + 1) * LANES], kg_ref[...])))
        vp = v[:, p * LANES:(p + 1) * LANES]
        v_ref[0, 2 * p] = with_ones(vp)
        v_ref[0, 2 * p + 1] = with_ones(pltpu.roll(vp, HEAD_DIM, 1))
    qi = _dot(hb, wqi_ref[...])
    for p in range(IDX_Q_W // LANES):
        put_heads(qi_ref, p, rope(qi[:, p * LANES:(p + 1) * LANES]))
    small = _dot(hb, ws_ref[...])
    ki_ref[0] = rope(small)[:, :IDX_DIM].astype(ki_ref.dtype)
    wi_ref[0] = small[:, IDX_DIM:IDX_DIM + IDX_HEADS] * (IDX_HEADS ** -0.5 * IDX_DIM ** -0.5)


def _dsa_proj(h, ng, w_in, q_g, k_g):
    b, tp, d = h.shape
    tm = SEQ_ALIGN
    o0, o1, o2, o3, o4 = (0, ATT_Q_W, ATT_Q_W + ATT_KV_W, ATT_Q_W + 2 * ATT_KV_W,
                          ATT_Q_W + 2 * ATT_KV_W + IDX_Q_W)
    w = w_in.astype(_MXU)
    w_small = jnp.pad(w[:, o4:], ((0, 0), (0, LANES - (w.shape[1] - o4))))
    ws = [w[:, o0:o1], w[:, o1:o2], w[:, o2:o3], w[:, o3:o4], w_small]
    tables = _rope_tables(tp)
    pair = lambda gvec: jnp.concatenate([gvec, gvec]).reshape(1, LANES)
    heads = lambda n: (jax.ShapeDtypeStruct((b, n, tp, HEAD_DIM), _MXU),
                       pl.BlockSpec((1, n, tm, HEAD_DIM), lambda bi, i: (bi, 0, i, 0)))
    outs = [heads(ATT_HEADS), heads(ATT_KV_HEADS),
            (jax.ShapeDtypeStruct((b, ATT_KV_HEADS, tp, LANES), _MXU),
             pl.BlockSpec((1, ATT_KV_HEADS, tm, LANES), lambda bi, i: (bi, 0, i, 0))),
            heads(IDX_HEADS),
            (jax.ShapeDtypeStruct((b, tp, IDX_DIM), _MXU), pl.BlockSpec((1, tm, IDX_DIM), lambda bi, i: (bi, i, 0))),
            (jax.ShapeDtypeStruct((b, tp, IDX_HEADS), _F32), pl.BlockSpec((1, tm, IDX_HEADS), lambda bi, i: (bi, i, 0)))]
    tab = pl.BlockSpec((tm, LANES), lambda bi, i: (i, 0))
    return pl.pallas_call(
        _dsa_proj_body,
        out_shape=[o[0] for o in outs],
        grid=(b, tp // tm),
        in_specs=[pl.BlockSpec((1, tm, d), lambda bi, i: (bi, i, 0)), _const_spec((1, d))]
                 + [_const_spec(x.shape) for x in ws]
                 + [_const_spec((1, LANES)), _const_spec((1, LANES)), tab, tab, tab],
        out_specs=[o[1] for o in outs],
        compiler_params=_params("parallel", "parallel"),
        name="dsa_proj",
    )(h, ng.reshape(1, d), *ws, pair(q_g), pair(k_g), *tables)


def _dsa_attn_body(qi_ref, wi_ref, ki_ref, q_ref, k_ref, v_ref, o_ref,
                   keys_ref, bias_ref, wb_ref, m_ref, acc_ref, thr_ref, cnt_ref, *, topk, widths):
    tq = SEQ_ALIGN
    tk = SEQ_ALIGN
    i = pl.program_id(1)
    n_tiles = i + 1
    n_key_tiles = keys_ref.shape[1] // tk

    keys_ref[...] = jnp.full(keys_ref.shape, INT_MIN, jnp.int32)
    wi = wi_ref[0]
    for hd in range(IDX_HEADS):
        wb_ref[hd] = jnp.broadcast_to(wi[:, hd:hd + 1], (tq, LANES))
    qi = qi_ref[0].reshape(IDX_HEADS * tq, IDX_DIM)
    row = lax.broadcasted_iota(jnp.int32, (tq, tk), 0)
    col = lax.broadcasted_iota(jnp.int32, (tq, tk), 1)

    def score_tiles(starts):
        zs = [_dot_nt(qi, ki_ref[0, pl.ds(start, tk), :]) for start in starts]
        for start, z in zip(starts, zs):
            score = jnp.zeros((tq, tk), _F32)
            for hd in range(IDX_HEADS):
                wb = wb_ref[hd]
                score = score + jnp.maximum(z[hd * tq:(hd + 1) * tq], 0.0) * jnp.concatenate([wb, wb], axis=1)
            bits = pltpu.bitcast(score, jnp.int32)
            key = bits ^ ((bits >> 31) & 0x7FFFFFFF)
            key = jnp.where(bits == INT_MIN, 0, key)
            key = jnp.maximum(key, INT_MIN + 1)
            causal = (start + col) <= (i * tq + row)
            keys_ref[:, pl.ds(start, tk)] = jnp.where(causal, key, INT_MIN)

    def score_pair(j, carry):
        start = pl.multiple_of(j * (2 * tk), 2 * tk)
        score_tiles([start, start + tk])
        return carry

    lax.fori_loop(0, n_tiles // 2, score_pair, 0)

    @pl.when(n_tiles % 2 == 1)
    def _():
        score_tiles([pl.multiple_of(i * tk, tk)])

    def count(mask):
        return jnp.sum(mask.astype(jnp.int32), axis=-1, keepdims=True)

    def select(width, cover):
        kv = keys_ref[:, :width]

        def value_bit(it, thr):
            cand = thr + (jnp.int32(1) << (31 - it))
            return jnp.where(count(kv >= cand) >= topk, cand, thr)

        thr = lax.fori_loop(0, 32, value_bit, jnp.full((tq, 1), INT_MIN, jnp.int32))
        floor = jnp.maximum(thr, INT_MIN + 1)
        picked = kv >= floor
        bias_ref[:, :width] = jnp.where(picked, 0.0, NEG_BIG)
        if cover > width:
            bias_ref[:, width:cover] = jnp.full((tq, cover - width), NEG_BIG, _F32)
        thr_ref[...] = thr
        cnt_ref[...] = count(picked)

    lo = 0
    for n_q, width, cover in widths:
        @pl.when((i >= lo) & (i < n_q))
        def _(width=width, cover=cover):
            select(width, cover)
        lo = n_q

    def break_ties(g, carry):
        rows8 = pl.ds(pl.multiple_of(g * 8, 8), 8)

        @pl.when(jnp.max(cnt_ref[rows8, :]) > topk)
        def _():
            kv = keys_ref[rows8, :]
            thr = thr_ref[rows8, :]
            above = kv > thr
            equal = (kv == thr) & (thr > INT_MIN)
            need = topk - count(above)
            pos = lax.broadcasted_iota(jnp.int32, kv.shape, 1)

            def pos_bit(it, last_short):
                cand = last_short + (jnp.int32(1) << (12 - it))
                return jnp.where(count(equal & (pos <= cand)) < need, cand, last_short)

            limit = lax.fori_loop(0, 13, pos_bit, jnp.full((8, 1), -1, jnp.int32)) + 1
            bias_ref[rows8, :] = jnp.where(above | (equal & (pos <= limit)), 0.0, NEG_BIG)
        return carry

    @pl.when(jnp.max(cnt_ref[...]) > topk)
    def _():
        lax.fori_loop(0, tq // 8, break_ties, 0)

    rows = ATT_GROUP * ATT_ROWS
    tiles_per_span = ATT_SPAN // tk
    span_tiles = (n_key_tiles // tiles_per_span) * tiles_per_span
    n_spans = (jnp.minimum(n_tiles, span_tiles) + tiles_per_span - 1) // tiles_per_span
    chains = [(gl, qh) for gl in range(ATT_GROUPS_PER_STEP) for qh in range(tq // ATT_ROWS)]

    def group_step(gs, carry):
        kv = [gs * ATT_GROUPS_PER_STEP + gl for gl, _ in chains]
        q_rows = [slice(qh * ATT_ROWS, (qh + 1) * ATT_ROWS) for _, qh in chains]
        qs = [q_ref[0, pl.ds(kv[c] * ATT_GROUP, ATT_GROUP), q_rows[c], :].reshape(rows, HEAD_DIM)
              for c in range(len(chains))]
        m_ref[...] = jnp.full(m_ref.shape, NEG_BIG, _F32)
        acc_ref[...] = jnp.zeros(acc_ref.shape, _F32)

        def flash(start, width):
            s = [_dot_nt(qs[c], k_ref[0, kv[c], pl.ds(start, width), :]) for c in range(len(chains))]
            p = []
            for c in range(len(chains)):
                bias = bias_ref[q_rows[c], pl.ds(start, width)]
                sc = (s[c].reshape(ATT_GROUP, ATT_ROWS, width) + bias[None]).reshape(rows, width)
                m_prev = m_ref[c]
                m_new = jnp.maximum(m_prev, jnp.max(sc, axis=-1, keepdims=True))
                acc_ref[c] = jnp.exp2(m_prev - m_new) * acc_ref[c]
                m_ref[c] = m_new
                p.append(jnp.exp2(sc - m_new).astype(_MXU))
            for c in range(len(chains)):
                acc_ref[c] += jnp.dot(p[c], v_ref[0, kv[c], pl.ds(start, width), :],
                                      preferred_element_type=_F32)

        def span_step(j, carry):
            flash(pl.multiple_of(j * ATT_SPAN, ATT_SPAN), ATT_SPAN)
            return carry

        lax.fori_loop(0, n_spans, span_step, 0)
        if n_key_tiles > span_tiles:
            @pl.when(n_tiles > span_tiles)
            def _():
                flash(span_tiles * tk, (n_key_tiles - span_tiles) * tk)
        for c in range(len(chains)):
            acc = acc_ref[c]
            out = acc[:, :HEAD_DIM] / acc[:, HEAD_DIM:HEAD_DIM + 1]
            o_ref[0, pl.ds(kv[c] * ATT_GROUP, ATT_GROUP), q_rows[c], :] = (
                out.reshape(ATT_GROUP, ATT_ROWS, HEAD_DIM).astype(o_ref.dtype))
        return carry

    lax.fori_loop(0, ATT_KV_HEADS // ATT_GROUPS_PER_STEP, group_step, 0)


def _dsa_attn(q, k, v, qi, ki, wi, topk):
    b, _, tp, _ = q.shape
    tq = SEQ_ALIGN
    nq = tp // tq
    tiles_per_span = ATT_SPAN // tq
    classes = sorted(set(range(SELECT_CLASS_TILES, nq + 1, SELECT_CLASS_TILES)) | {nq})
    widths = tuple((n, n * tq, min(-(-n // tiles_per_span) * tiles_per_span, nq) * tq) for n in classes)
    n_chains = ATT_GROUPS_PER_STEP * (tq // ATT_ROWS)
    per_q = lambda n: pl.BlockSpec((1, n, tq, HEAD_DIM), lambda bi, i: (bi, 0, i, 0))
    per_b = lambda n, w: pl.BlockSpec((1, n, tp, w), lambda bi, i: (bi, 0, 0, 0),
                                      pipeline_mode=pl.Buffered(1))
    return pl.pallas_call(
        functools.partial(_dsa_attn_body, topk=topk, widths=widths),
        out_shape=jax.ShapeDtypeStruct((b, ATT_HEADS, tp, HEAD_DIM), _MXU),
        grid=(b, nq),
        in_specs=[per_q(IDX_HEADS),
                  pl.BlockSpec((1, tq, IDX_HEADS), lambda bi, i: (bi, i, 0)),
                  pl.BlockSpec((1, tp, IDX_DIM), lambda bi, i: (bi, 0, 0), pipeline_mode=pl.Buffered(1)),
                  per_q(ATT_HEADS), per_b(ATT_KV_HEADS, HEAD_DIM), per_b(ATT_KV_HEADS, LANES)],
        out_specs=per_q(ATT_HEADS),
        scratch_shapes=[pltpu.VMEM((tq, tp), jnp.int32),
                        pltpu.VMEM((tq, tp), _F32),
                        pltpu.VMEM((IDX_HEADS, tq, LANES), _F32),
                        pltpu.VMEM((n_chains, ATT_GROUP * ATT_ROWS, 1), _F32),
                        pltpu.VMEM((n_chains, ATT_GROUP * ATT_ROWS, LANES), _F32),
                        pltpu.VMEM((tq, 1), jnp.int32),
                        pltpu.VMEM((tq, 1), jnp.int32)],
        compiler_params=_params("parallel", "arbitrary"),
        name="dsa_attn",
    )(qi, wi, ki, q, k, v)


def _dsa_out_body(h_ref, y_ref, w_ref, o_ref):
    y = jnp.concatenate([y_ref[0, hd] for hd in range(ATT_HEADS)], axis=1)
    o_ref[0] = h_ref[0] + _dot(y, w_ref[...])


def _dsa_out(h, y, w_o):
    b, tp, d = h.shape
    tm = SEQ_ALIGN
    tile = pl.BlockSpec((1, tm, d), lambda bi, i: (bi, i, 0))
    return pl.pallas_call(
        _dsa_out_body,
        out_shape=jax.ShapeDtypeStruct((b, tp, d), _F32),
        grid=(b, tp // tm),
        in_specs=[tile, pl.BlockSpec((1, ATT_HEADS, tm, HEAD_DIM), lambda bi, i: (bi, 0, i, 0)),
                  _const_spec(w_o.shape)],
        out_specs=tile,
        compiler_params=_params("parallel", "parallel"),
        name="dsa_out",
    )(h, y, w_o.astype(_MXU))


def _dsa_mix(h, ng, w_in, q_g, k_g, topk):
    q, k, v, qi, ki, wi = _dsa_proj(h, ng, w_in, q_g, k_g)
    return _dsa_attn(q, k, v, qi, ki, wi, topk)


def _dsa_layer(h, ng, w_in, q_g, k_g, w_o, topk):
    return _dsa_out(h, _dsa_mix(h, ng, w_in, q_g, k_g, topk), w_o)


def kernel(x, meta, norm_g, ffn_w_in, ffn_w_out, rk_mix, rk_w_r, rk_w_k, rk_w_v, rk_w_o, rk_w0, rk_w1,
           rk_w2, rk_a0, rk_a1, rk_a2, rk_g1, rk_g2, rk_k_k, rk_k_a, rk_r_k, rk_lnx_g, rk_lnx_b,
           at_w_in, at_q_g, at_k_g, at_w_o):
    b, seq, d = x.shape
    topk = min(TOPK_MAX, seq // 4)
    t = seq + N_META
    tp = -(-t // SEQ_ALIGN) * SEQ_ALIGN
    h = jnp.concatenate([jnp.broadcast_to(meta.astype(x.dtype)[None], (b, N_META, d)), x,
                         jnp.zeros((b, tp - t, d), x.dtype)], axis=1)
    rk = dict(mix=rk_mix, w_r=rk_w_r, w_k=rk_w_k, w_v=rk_w_v, w_o=rk_w_o, w0=rk_w0, w1=rk_w1, w2=rk_w2,
              a0=rk_a0, a1=rk_a1, a2=rk_a2, g1=rk_g1, g2=rk_g2, k_k=rk_k_k, k_a=rk_k_a, r_k=rk_r_k,
              lnx_g=rk_lnx_g, lnx_b=rk_lnx_b)
    n_mixers = 2
    depth = norm_g.shape[0]
    flat = lambda a: a.reshape(b * tp, d)
    for layer in range(depth):
        g = norm_g[layer]
        last = layer + 1 == depth
        w_in2, w_out2 = ffn_w_in[layer, 1], ffn_w_out[layer, 1]
        h = _ffn(flat(h), g[0], ffn_w_in[layer, 0], ffn_w_out[layer, 0]).reshape(b, tp, d)
        j = layer // n_mixers
        if layer % n_mixers == 0:
            p = {name: val[j] for name, val in rk.items()}
            if last:
                out = _ffn_tail(_rwkv_layer(h, g[1], p), g[2], w_in2, w_out2, seq)
            else:
                y, gate = _rwkv_mix(h, g[1], p)
                h = _ffn(flat(h), g[2], w_in2, w_out2, gated=(flat(y), flat(gate), p["w_o"])).reshape(b, tp, d)
        else:
            if last:
                y = _dsa_mix(h, g[1], at_w_in[j], at_q_g[j], at_k_g[j], topk)
                out = _ffn_tail(h, g[2], w_in2, w_out2, seq, heads=(y, at_w_o[j]))
            else:
                h = _dsa_layer(h, g[1], at_w_in[j], at_q_g[j], at_k_g[j], at_w_o[j], topk)
                h = _ffn(flat(h), g[2], w_in2, w_out2).reshape(b, tp, d)
    return out
```

```python
import functools

import jax
import jax.numpy as jnp
from jax import lax
from jax.experimental import pallas as pl
from jax.experimental.pallas import tpu as pltpu

D_MODEL = 1024
N_META = 16
D_FF = 2816
NORM_EPS = 1e-6
HEAD_DIM = 64
ROPE_DIM = HEAD_DIM // 4
ROPE_THETA = 500000.0
RWKV_HEADS = D_MODEL // HEAD_DIM
RWKV_LN_EPS = 64e-5
ATT_HEADS = D_MODEL // HEAD_DIM
ATT_KV_HEADS = 4
ATT_GROUP = ATT_HEADS // ATT_KV_HEADS
IDX_HEADS = 8
IDX_DIM = 64
TOPK_MAX = 256
ATT_Q_W = ATT_HEADS * HEAD_DIM
ATT_KV_W = ATT_KV_HEADS * HEAD_DIM
IDX_Q_W = IDX_HEADS * IDX_DIM

LANES = 128
SEQ_ALIGN = 256
VMEM_LIMIT = 56 * 1024 * 1024
FF_CHUNK = 256
WKV_CHUNK = 64
WKV_CHUNKS_PER_STEP = 4
WKV_INV_BASE = 4
WKV_PAIRS_PER_STEP = 8
ATT_ROWS = 128
ATT_SPAN = 1024
ATT_GROUPS_PER_STEP = 2
SELECT_CLASS_TILES = 1
INT_MIN = -(2 ** 31)
NEG_BIG = -1e30
LOG2_E = 1.4426950408889634

_MXU = jnp.bfloat16
_F32 = jnp.float32


def _dot(a, b):
    return jnp.dot(a.astype(_MXU), b.astype(_MXU), preferred_element_type=_F32)


def _dot_nt(a, b):
    return lax.dot_general(a.astype(_MXU), b.astype(_MXU), (((1,), (1,)), ((), ())),
                           preferred_element_type=_F32)


def _dot_tn(a, b):
    return lax.dot_general(a.astype(_MXU), b.astype(_MXU), (((0,), (0,)), ((), ())),
                           preferred_element_type=_F32)


def _split2(x):
    hi = x.astype(_MXU)
    lo = (x - hi.astype(_F32)).astype(_MXU)
    return hi, lo


def _split3(x):
    hi = x.astype(_MXU)
    r1 = x - hi.astype(_F32)
    mid = r1.astype(_MXU)
    lo = (r1 - mid.astype(_F32)).astype(_MXU)
    return hi, mid, lo


def _rms(x, eps=NORM_EPS):
    return x * lax.rsqrt(jnp.mean(x * x, axis=-1, keepdims=True) + eps)


def _sigmoid(x):
    return 1.0 / (1.0 + jnp.exp(-x))


def _params(*sem):
    return pltpu.CompilerParams(dimension_semantics=sem, vmem_limit_bytes=VMEM_LIMIT)


def _const_spec(shape):
    n = len(shape)
    return pl.BlockSpec(shape, lambda *_: (0,) * n, pipeline_mode=pl.Buffered(1))


def _swiglu_residual(x, g_ref, win_ref, wout_ref):
    xb = (_rms(x) * g_ref[...]).astype(_MXU)
    acc = jnp.zeros(x.shape, _F32)
    for c in range(D_FF // FF_CHUNK):
        lo = c * FF_CHUNK
        gate = _dot(xb, win_ref[:, lo:lo + FF_CHUNK])
        up = _dot(xb, win_ref[:, D_FF + lo:D_FF + lo + FF_CHUNK])
        act = gate * _sigmoid(gate) * up
        acc = acc + _dot(act, wout_ref[lo:lo + FF_CHUNK, :])
    return x + 0.5 * acc


def _ffn_body(h_ref, g_ref, win_ref, wout_ref, o_ref):
    o_ref[...] = _swiglu_residual(h_ref[...], g_ref, win_ref, wout_ref)


def _ffn_gated_body(h_ref, y_ref, gate_ref, wo_ref, g_ref, win_ref, wout_ref, o_ref):
    x = h_ref[...] + _dot(y_ref[...] * gate_ref[...], wo_ref[...])
    o_ref[...] = _swiglu_residual(x, g_ref, win_ref, wout_ref)


def _shifted(ref, next_ref):
    return jnp.concatenate([ref[N_META:, :], next_ref[...]], axis=0)


def _ffn_tail_body(h_ref, hn_ref, g_ref, win_ref, wout_ref, o_ref):
    x = _shifted(h_ref.at[0], hn_ref.at[0])
    o_ref[0] = _swiglu_residual(x, g_ref, win_ref, wout_ref)


def _ffn_tail_heads_body(h_ref, hn_ref, y_ref, yn_ref, wo_ref, g_ref, win_ref, wout_ref, o_ref):
    y = jnp.concatenate([_shifted(y_ref.at[0, hd], yn_ref.at[0, hd]) for hd in range(ATT_HEADS)], axis=1)
    x = _shifted(h_ref.at[0], hn_ref.at[0]) + _dot(y, wo_ref[...])
    o_ref[0] = _swiglu_residual(x, g_ref, win_ref, wout_ref)


def _ffn(h2, g, w_in, w_out, gated=None):
    rows, d = h2.shape
    tm = 512 if rows % 512 == 0 else SEQ_ALIGN
    win, wout = w_in.astype(_MXU), w_out.astype(_MXU)
    tile = pl.BlockSpec((tm, d), lambda i: (i, 0))
    acts, specs = [h2], [tile]
    if gated is not None:
        y2, gate2, w_o = gated
        acts += [y2, gate2, w_o.astype(_MXU)]
        specs += [tile, tile, _const_spec(w_o.shape)]
    return pl.pallas_call(
        _ffn_body if gated is None else _ffn_gated_body,
        out_shape=jax.ShapeDtypeStruct((rows, d), _F32),
        grid=(rows // tm,),
        in_specs=specs + [_const_spec((1, d)), _const_spec(win.shape), _const_spec(wout.shape)],
        out_specs=tile,
        compiler_params=_params("parallel"),
        name="ffn" if gated is None else "ffn_gated",
    )(*acts, g.reshape(1, d), win, wout)


def _ffn_tail(h, g, w_in, w_out, seq, heads=None):
    b, tp, d = h.shape
    tm = next((t for t in (512, SEQ_ALIGN) if seq % t == 0), None)
    if tm is None:
        if heads is not None:
            h = _dsa_out(h, *heads)
        return _ffn(h.reshape(b * tp, d), g, w_in, w_out).reshape(b, tp, d)[:, N_META:N_META + seq]
    win, wout = w_in.astype(_MXU), w_out.astype(_MXU)
    nxt = lambda bi, i: (bi, (i + 1) * (tm // N_META), 0)
    acts = [h, h]
    specs = [pl.BlockSpec((1, tm, d), lambda bi, i: (bi, i, 0)), pl.BlockSpec((1, N_META, d), nxt)]
    if heads is not None:
        y, w_o = heads
        acts += [y, y, w_o.astype(_MXU)]
        specs += [pl.BlockSpec((1, ATT_HEADS, tm, HEAD_DIM), lambda bi, i: (bi, 0, i, 0)),
                  pl.BlockSpec((1, ATT_HEADS, N_META, HEAD_DIM), lambda bi, i: (bi, 0, (i + 1) * (tm // N_META), 0)),
                  _const_spec(w_o.shape)]
    return pl.pallas_call(
        _ffn_tail_body if heads is None else _ffn_tail_heads_body,
        out_shape=jax.ShapeDtypeStruct((b, seq, d), _F32),
        grid=(b, seq // tm),
        in_specs=specs + [_const_spec((1, d)), _const_spec(win.shape), _const_spec(wout.shape)],
        out_specs=pl.BlockSpec((1, tm, d), lambda bi, i: (bi, i, 0)),
        compiler_params=_params("parallel", "parallel"),
        name="ffn_tail",
    )(*acts, g.reshape(1, d), win, wout)


def _rwkv_prep_body(h_ref, hprev_ref, ng_ref, mix_ref, wr_ref, wk_ref, wv_ref, w1_ref, w2_ref,
                    a1_ref, a2_ref, g1_ref, g2_ref, w0_ref, a0_ref,
                    r_ref, k_ref, v_ref, lw_ref, a_ref, g_ref):
    ng = ng_ref[...]
    hn = _rms(h_ref[0]) * ng
    prev = (_rms(hprev_ref[0]) * ng)[7:8, :]
    prev = jnp.where(pl.program_id(1) == 0, 0.0, prev)
    row = lax.broadcasted_iota(jnp.int32, hn.shape, 0)
    shifted = jnp.where(row == 0, prev, pltpu.roll(hn, 1, 0))
    xx = shifted - hn

    def mixed(j):
        return (hn + xx * mix_ref[j:j + 1, :]).astype(_MXU)

    r_ref[0] = _dot(mixed(0), wr_ref[...])
    k_ref[0] = _dot(mixed(2), wk_ref[...])
    v_ref[0] = _dot(mixed(3), wv_ref[...])
    wl = w0_ref[...] + _dot(jnp.tanh(_dot(mixed(1), w1_ref[...])), w2_ref[...])
    lw_ref[0] = -0.6065306597126334 * _sigmoid(wl)
    a_ref[0] = _sigmoid(a0_ref[...] + _dot(_dot(mixed(4), a1_ref[...]), a2_ref[...]))
    g_ref[0] = _dot(_sigmoid(_dot(mixed(5), g1_ref[...])), g2_ref[...])


def _rwkv_prep(h, ng, mix, w_r, w_k, w_v, w1, w2, a1, a2, g1, g2, w0, a0):
    b, tp, d = h.shape
    tm = SEQ_ALIGN
    ws = [w.astype(_MXU) for w in (w_r, w_k, w_v, w1, w2, a1, a2, g1, g2)]
    act = pl.BlockSpec((1, tm, d), lambda bi, i: (bi, i, 0))
    prev = pl.BlockSpec((1, 8, d), lambda bi, i: (bi, jnp.maximum(i * (tm // 8) - 1, 0), 0))
    out = jax.ShapeDtypeStruct((b, tp, d), _F32)
    return pl.pallas_call(
        _rwkv_prep_body,
        out_shape=[out] * 6,
        grid=(b, tp // tm),
        in_specs=[act, prev, _const_spec((1, d)), _const_spec(mix.shape)]
                 + [_const_spec(w.shape) for w in ws]
                 + [_const_spec((1, d)), _const_spec((1, d))],
        out_specs=[act] * 6,
        compiler_params=_params("parallel", "parallel"),
        name="rwkv_prep",
    )(h, h, ng.reshape(1, d), mix, *ws, w0.reshape(1, d), a0.reshape(1, d))


def _wkv_body(r_ref, k_ref, v_ref, lw_ref, a_ref, kk_ref, ka_ref, rk_ref, lg_ref, lb_ref,
              o_ref, s_ref, *, n_chunks, n_pairs):
    c_len = WKV_CHUNK

    @pl.when(pl.program_id(2) == 0)
    def _():
        s_ref[...] = jnp.zeros(s_ref.shape, _F32)

    lane = lax.broadcasted_iota(jnp.int32, (c_len, LANES), 1)
    row = lax.broadcasted_iota(jnp.int32, (c_len, LANES), 0)
    first = lane < HEAD_DIM
    col = jnp.where(first, lane, lane - HEAD_DIM)
    strict = col < row
    incl = col <= row
    sq_r = lax.broadcasted_iota(jnp.int32, (LANES, LANES), 0)
    sq_c = lax.broadcasted_iota(jnp.int32, (LANES, LANES), 1)
    same_head = (sq_r < HEAD_DIM) == (sq_c < HEAD_DIM)
    eye = sq_r == sq_c
    tri_r = lax.broadcasted_iota(jnp.int32, (c_len, 3 * c_len), 0)
    tri_c = lax.broadcasted_iota(jnp.int32, (c_len, 3 * c_len), 1) & (c_len - 1)
    tri = jnp.where(tri_c <= tri_r, 1.0, 0.0).astype(_MXU)

    def head_sum(x):
        s0 = jnp.sum(jnp.where(first, x, 0.0), axis=-1, keepdims=True)
        s1 = jnp.sum(jnp.where(first, 0.0, x), axis=-1, keepdims=True)
        return jnp.where(first, s0, s1)

    def blockdiag(x):
        f = first if x.shape[1] == LANES else jnp.concatenate([first] * (x.shape[1] // LANES), axis=1)
        xb = x.astype(_MXU)
        zero = jnp.zeros_like(xb)
        return jnp.concatenate([jnp.where(f, xb, zero), jnp.where(f, zero, xb)], axis=0)

    def apply(p, x):
        return jnp.dot(p.astype(_MXU), blockdiag(x), preferred_element_type=_F32)

    units = [(c, p) for c in range(n_chunks) for p in range(n_pairs)]
    st = {}
    for u in units:
        c, p = u
        sl = pl.ds(c * c_len, c_len)
        ln = slice(p * LANES, (p + 1) * LANES)
        r = r_ref[0, sl, ln]
        k_raw = k_ref[0, sl, ln]
        v = v_ref[0, sl, ln]
        lw = lw_ref[0, sl, ln]
        gate = a_ref[0, sl, ln]
        kk = k_raw * kk_ref[:, ln]
        kk = kk / jnp.maximum(jnp.sqrt(head_sum(kk * kk)), 1e-12)
        k = k_raw * (1.0 + (gate - 1.0) * ka_ref[:, ln])
        cs = jnp.dot(tri, jnp.concatenate(_split3(lw), axis=0), preferred_element_type=_F32)
        st[u] = dict(r=r, k=k, v=v, lw=lw, cs=cs, a_vec=-kk, b_vec=kk * gate)

    for u in units:
        d = st[u]
        cs, lw = d["cs"], d["lw"]
        cs_prev = cs - lw
        c_mid = cs[c_len // 2 - 1:c_len // 2, :]
        c_end = cs[c_len - 1:c_len, :]
        inv_rel = jnp.exp(c_mid - cs)
        to_end = jnp.exp(c_end - cs)
        d["a_abs"] = d["a_vec"] * jnp.exp(cs_prev)
        d["r_abs"] = d["r"] * jnp.exp(cs)
        d["b_end"] = d["b_vec"] * to_end
        d["k_end"] = d["k"] * to_end
        d["g_end"] = jnp.exp(c_end)
        lhs = jnp.concatenate([d["a_vec"] * jnp.exp(cs_prev - c_mid), d["r"] * jnp.exp(cs - c_mid)],
                              axis=0).astype(_MXU)
        d["sb"] = lax.dot_general(lhs, blockdiag(d["b_vec"] * inv_rel), (((1,), (1,)), ((), ())),
                                  preferred_element_type=_F32)
        d["sk"] = lax.dot_general(lhs, blockdiag(d["k"] * inv_rel), (((1,), (1,)), ((), ())),
                                  preferred_element_type=_F32)

    base_blk = (row // WKV_INV_BASE) == (col // WKV_INV_BASE)
    for u in units:
        d = st[u]
        d["l"] = jnp.where(strict, d["sb"][:c_len], 0.0)
        d["g_rb"] = jnp.where(incl, d["sb"][c_len:], 0.0)
        d["g_rk"] = jnp.where(incl, d["sk"][c_len:], 0.0)
        l_ak = jnp.where(strict, d["sk"][:c_len], 0.0)
        d["x0"] = jnp.concatenate([d["a_abs"], apply(l_ak, d["v"])], axis=1)
        l_base = jnp.where(base_blk, d["l"], 0.0)
        d["t"] = jnp.where(col == row, 1.0, l_base)
        d["q"] = apply(l_base, l_base)
    for u in units:
        d = st[u]
        d["t"] = d["t"] + apply(d["t"], d["q"])
    width = WKV_INV_BASE
    while width < c_len:
        off_blk = ((row // (2 * width)) == (col // (2 * width))) & ((row // width) != (col // width))
        for u in units:
            d = st[u]
            d["q"] = apply(jnp.where(off_blk, d["l"], 0.0), d["t"])
        for u in units:
            d = st[u]
            d["t"] = d["t"] + apply(d["t"], d["q"])
        width *= 2
    for u in units:
        d = st[u]
        d["x"] = apply(d["t"], d["x0"])

    for u in units:
        d = st[u]
        x = d["x"]
        w_a = x[:, :LANES]
        u0 = x[:, LANES:]
        gx = apply(d["g_rb"], x)
        d["q_eff"] = d["r_abs"] + gx[:, :LANES]
        d["y0"] = gx[:, LANES:] + apply(d["g_rk"], d["v"])
        d["m"] = jnp.where(same_head, _dot_tn(w_a, d["b_end"]), 0.0) + jnp.where(eye, d["g_end"], 0.0)
        d["n"] = jnp.where(same_head, _dot_tn(u0, d["b_end"]) + _dot_tn(d["v"], d["k_end"]), 0.0)

    state = [s_ref[p] for p in range(n_pairs)]
    for u in units:
        c, p = u
        d = st[u]
        s_hi, s_lo = _split2(state[p])
        qb = d["q_eff"].astype(_MXU)
        d["y"] = d["y0"] + _dot_nt(jnp.concatenate([qb, qb], axis=1), jnp.concatenate([s_hi, s_lo], axis=1))
        m_hi, m_lo = _split2(d["m"])
        state[p] = jnp.dot(jnp.concatenate([s_hi, s_hi, s_lo], axis=1),
                           jnp.concatenate([m_hi, m_lo, m_hi], axis=0),
                           preferred_element_type=_F32) + d["n"]
    for p in range(n_pairs):
        s_ref[p] = state[p]

    for u in units:
        c, p = u
        d = st[u]
        ln = slice(p * LANES, (p + 1) * LANES)
        y = d["y"]
        mu = head_sum(y) * (1.0 / HEAD_DIM)
        yc = y - mu
        var = head_sum(yc * yc) * (1.0 / HEAD_DIM)
        yn = yc * lax.rsqrt(var + RWKV_LN_EPS) * lg_ref[:, ln] + lb_ref[:, ln]
        o_ref[0, pl.ds(c * c_len, c_len), ln] = yn + head_sum(d["r"] * d["k"] * rk_ref[:, ln]) * d["v"]


def _wkv(r, k, v, lw, a, k_k, k_a, r_k, ln_g, ln_b):
    b, tp, d = r.shape
    tc = WKV_CHUNK * WKV_CHUNKS_PER_STEP
    width = LANES * WKV_PAIRS_PER_STEP
    act = pl.BlockSpec((1, tc, width), lambda bi, p, s: (bi, s, p))
    par = pl.BlockSpec((1, width), lambda bi, p, s: (0, p))
    return pl.pallas_call(
        functools.partial(_wkv_body, n_chunks=WKV_CHUNKS_PER_STEP, n_pairs=WKV_PAIRS_PER_STEP),
        out_shape=jax.ShapeDtypeStruct((b, tp, d), _F32),
        grid=(b, d // width, tp // tc),
        in_specs=[act] * 5 + [par] * 5,
        out_specs=act,
        scratch_shapes=[pltpu.VMEM((WKV_PAIRS_PER_STEP, LANES, LANES), _F32)],
        compiler_params=_params("parallel", "parallel", "arbitrary"),
        name="wkv",
    )(r, k, v, lw, a, *(p.reshape(1, d) for p in (k_k, k_a, r_k, ln_g, ln_b)))


def _gated_out_body(h_ref, y_ref, g_ref, w_ref, o_ref):
    o_ref[...] = h_ref[...] + _dot(y_ref[...] * g_ref[...], w_ref[...])


def _gated_out(h2, y2, gate2, w):
    rows, d = h2.shape
    tm = 512 if rows % 512 == 0 else SEQ_ALIGN
    tile = pl.BlockSpec((tm, d), lambda i: (i, 0))
    return pl.pallas_call(
        _gated_out_body,
        out_shape=jax.ShapeDtypeStruct((rows, d), _F32),
        grid=(rows // tm,),
        in_specs=[tile] * 3 + [_const_spec(w.shape)],
        out_specs=tile,
        compiler_params=_params("parallel"),
        name="rwkv_out",
    )(h2, y2, gate2, w.astype(_MXU))


def _rwkv_mix(h, ng, p):
    r, k, v, lw, a, g = _rwkv_prep(h, ng, p["mix"], p["w_r"], p["w_k"], p["w_v"], p["w1"], p["w2"],
                                   p["a1"], p["a2"], p["g1"], p["g2"], p["w0"], p["a0"])
    return _wkv(r, k, v, lw, a, p["k_k"], p["k_a"], p["r_k"], p["lnx_g"], p["lnx_b"]), g


def _rwkv_layer(h, ng, p):
    b, tp, d = h.shape
    y, g = _rwkv_mix(h, ng, p)
    flat = lambda t: t.reshape(b * tp, d)
    return _gated_out(flat(h), flat(y), flat(g), p["w_o"]).reshape(b, tp, d)


def _rope_tables(tp):
    half = ROPE_DIM // 2
    inv = ROPE_THETA ** (-jnp.arange(0, ROPE_DIM, 2, dtype=_F32) / ROPE_DIM)
    ang = jnp.arange(tp, dtype=_F32)[:, None] * inv[None, :]
    cos, sin = jnp.cos(ang), jnp.sin(ang)
    ones = jnp.ones((tp, HEAD_DIM - ROPE_DIM), _F32)
    zeros = jnp.zeros((tp, HEAD_DIM - ROPE_DIM), _F32)
    zh = jnp.zeros((tp, half), _F32)
    c = jnp.concatenate([cos, cos, ones], axis=1)
    s_up = jnp.concatenate([-sin, zh, zeros], axis=1)
    s_dn = jnp.concatenate([zh, sin, zeros], axis=1)
    return tuple(jnp.concatenate([t, t], axis=1) for t in (c, s_up, s_dn))


def _dsa_proj_body(h_ref, ng_ref, wq_ref, wk_ref, wv_ref, wqi_ref, ws_ref, qg_ref, kg_ref,
                   c_ref, su_ref, sd_ref, q_ref, k_ref, v_ref, qi_ref, ki_ref, wi_ref):
    hb = (_rms(h_ref[0]) * ng_ref[...]).astype(_MXU)
    c, s_up, s_dn = c_ref[...], su_ref[...], sd_ref[...]
    lane = lax.broadcasted_iota(jnp.int32, c.shape, 1)
    first = lane < HEAD_DIM

    def rope(x):
        return x * c + pltpu.roll(x, LANES - ROPE_DIM // 2, 1) * s_up + pltpu.roll(x, ROPE_DIM // 2, 1) * s_dn

    sq_r = lax.broadcasted_iota(jnp.int32, (2 * LANES, LANES), 0) & (LANES - 1)
    sq_c = lax.broadcasted_iota(jnp.int32, (2 * LANES, LANES), 1)
    head_ones = jnp.where((sq_r < HEAD_DIM) == (sq_c < HEAD_DIM), 1.0, 0.0).astype(_MXU)

    def head_rms(x, gain):
        ss = jnp.dot(jnp.concatenate(_split2(x * x), axis=1), head_ones, preferred_element_type=_F32)
        return x * lax.rsqrt(ss * (1.0 / HEAD_DIM) + NORM_EPS) * gain

    def put_heads(dst_ref, pair, x):
        dst_ref[0, 2 * pair] = x[:, :HEAD_DIM].astype(dst_ref.dtype)
        dst_ref[0, 2 * pair + 1] = x[:, HEAD_DIM:].astype(dst_ref.dtype)

    def with_ones(x):
        return jnp.where(first, x, jnp.where(lane == HEAD_DIM, 1.0, 0.0)).astype(v_ref.dtype)

    q = _dot(hb, wq_ref[...])
    for p in range(ATT_Q_W // LANES):
        blk = rope(head_rms(q[:, p * LANES:(p + 1) * LANES], qg_ref[...]))
        put_heads(q_ref, p, blk * (HEAD_DIM ** -0.5 * LOG2_E))
    k = _dot(hb, wk_ref[...])
    v = _dot(hb, wv_ref[...])
    for p in range(ATT_KV_W // LANES):
        put_heads(k_ref, p, rope(head_rms(k[:, p * LANES:(p + 1) * LANES], kg_ref[...])))
        vp = v[:, p * LANES:(p + 1) * LANES]
        v_ref[0, 2 * p] = with_ones(vp)
        v_ref[0, 2 * p + 1] = with_ones(pltpu.roll(vp, HEAD_DIM, 1))
    qi = _dot(hb, wqi_ref[...])
    for p in range(IDX_Q_W // LANES):
        put_heads(qi_ref, p, rope(qi[:, p * LANES:(p + 1) * LANES]))
    small = _dot(hb, ws_ref[...])
    ki_ref[0] = rope(small)[:, :IDX_DIM].astype(ki_ref.dtype)
    wi_ref[0] = small[:, IDX_DIM:IDX_DIM + IDX_HEADS] * (IDX_HEADS ** -0.5 * IDX_DIM ** -0.5)


def _dsa_proj(h, ng, w_in, q_g, k_g):
    b, tp, d = h.shape
    tm = SEQ_ALIGN
    o0, o1, o2, o3, o4 = (0, ATT_Q_W, ATT_Q_W + ATT_KV_W, ATT_Q_W + 2 * ATT_KV_W,
                          ATT_Q_W + 2 * ATT_KV_W + IDX_Q_W)
    w = w_in.astype(_MXU)
    w_small = jnp.pad(w[:, o4:], ((0, 0), (0, LANES - (w.shape[1] - o4))))
    ws = [w[:, o0:o1], w[:, o1:o2], w[:, o2:o3], w[:, o3:o4], w_small]
    tables = _rope_tables(tp)
    pair = lambda gvec: jnp.concatenate([gvec, gvec]).reshape(1, LANES)
    heads = lambda n: (jax.ShapeDtypeStruct((b, n, tp, HEAD_DIM), _MXU),
                       pl.BlockSpec((1, n, tm, HEAD_DIM), lambda bi, i: (bi, 0, i, 0)))
    outs = [heads(ATT_HEADS), heads(ATT_KV_HEADS),
            (jax.ShapeDtypeStruct((b, ATT_KV_HEADS, tp, LANES), _MXU),
             pl.BlockSpec((1, ATT_KV_HEADS, tm, LANES), lambda bi, i: (bi, 0, i, 0))),
            heads(IDX_HEADS),
            (jax.ShapeDtypeStruct((b, tp, IDX_DIM), _MXU), pl.BlockSpec((1, tm, IDX_DIM), lambda bi, i: (bi, i, 0))),
            (jax.ShapeDtypeStruct((b, tp, IDX_HEADS), _F32), pl.BlockSpec((1, tm, IDX_HEADS), lambda bi, i: (bi, i, 0)))]
    tab = pl.BlockSpec((tm, LANES), lambda bi, i: (i, 0))
    return pl.pallas_call(
        _dsa_proj_body,
        out_shape=[o[0] for o in outs],
        grid=(b, tp // tm),
        in_specs=[pl.BlockSpec((1, tm, d), lambda bi, i: (bi, i, 0)), _const_spec((1, d))]
                 + [_const_spec(x.shape) for x in ws]
                 + [_const_spec((1, LANES)), _const_spec((1, LANES)), tab, tab, tab],
        out_specs=[o[1] for o in outs],
        compiler_params=_params("parallel", "parallel"),
        name="dsa_proj",
    )(h, ng.reshape(1, d), *ws, pair(q_g), pair(k_g), *tables)


def _dsa_attn_body(qi_ref, wi_ref, ki_ref, q_ref, k_ref, v_ref, o_ref,
                   keys_ref, bias_ref, wb_ref, m_ref, acc_ref, thr_ref, cnt_ref, *, topk, widths):
    tq = SEQ_ALIGN
    tk = SEQ_ALIGN
    i = pl.program_id(1)
    n_tiles = i + 1
    n_key_tiles = keys_ref.shape[1] // tk

    keys_ref[...] = jnp.full(keys_ref.shape, INT_MIN, jnp.int32)
    wi = wi_ref[0]
    for hd in range(IDX_HEADS):
        wb_ref[hd] = jnp.broadcast_to(wi[:, hd:hd + 1], (tq, LANES))
    qi = qi_ref[0].reshape(IDX_HEADS * tq, IDX_DIM)
    row = lax.broadcasted_iota(jnp.int32, (tq, tk), 0)
    col = lax.broadcasted_iota(jnp.int32, (tq, tk), 1)

    def score_tiles(starts):
        zs = [_dot_nt(qi, ki_ref[0, pl.ds(start, tk), :]) for start in starts]
        for start, z in zip(starts, zs):
            score = jnp.zeros((tq, tk), _F32)
            for hd in range(IDX_HEADS):
                wb = wb_ref[hd]
                score = score + jnp.maximum(z[hd * tq:(hd + 1) * tq], 0.0) * jnp.concatenate([wb, wb], axis=1)
            bits = pltpu.bitcast(score, jnp.int32)
            key = bits ^ ((bits >> 31) & 0x7FFFFFFF)
            key = jnp.where(bits == INT_MIN, 0, key)
            key = jnp.maximum(key, INT_MIN + 1)
            causal = (start + col) <= (i * tq + row)
            keys_ref[:, pl.ds(start, tk)] = jnp.where(causal, key, INT_MIN)

    def score_pair(j, carry):
        start = pl.multiple_of(j * (2 * tk), 2 * tk)
        score_tiles([start, start + tk])
        return carry

    lax.fori_loop(0, n_tiles // 2, score_pair, 0)

    @pl.when(n_tiles % 2 == 1)
    def _():
        score_tiles([pl.multiple_of(i * tk, tk)])

    def count(mask):
        return jnp.sum(mask.astype(jnp.int32), axis=-1, keepdims=True)

    def select(width, cover):
        kv = keys_ref[:, :width]

        def value_bit(it, thr):
            cand = thr + (jnp.int32(1) << (31 - it))
            return jnp.where(count(kv >= cand) >= topk, cand, thr)

        thr = lax.fori_loop(0, 32, value_bit, jnp.full((tq, 1), INT_MIN, jnp.int32))
        floor = jnp.maximum(thr, INT_MIN + 1)
        picked = kv >= floor
        bias_ref[:, :width] = jnp.where(picked, 0.0, NEG_BIG)
        if cover > width:
            bias_ref[:, width:cover] = jnp.full((tq, cover - width), NEG_BIG, _F32)
        thr_ref[...] = thr
        cnt_ref[...] = count(picked)

    lo = 0
    for n_q, width, cover in widths:
        @pl.when((i >= lo) & (i < n_q))
        def _(width=width, cover=cover):
            select(width, cover)
        lo = n_q

    def break_ties(g, carry):
        rows8 = pl.ds(pl.multiple_of(g * 8, 8), 8)

        @pl.when(jnp.max(cnt_ref[rows8, :]) > topk)
        def _():
            kv = keys_ref[rows8, :]
            thr = thr_ref[rows8, :]
            above = kv > thr
            equal = (kv == thr) & (thr > INT_MIN)
            need = topk - count(above)
            pos = lax.broadcasted_iota(jnp.int32, kv.shape, 1)

            def pos_bit(it, last_short):
                cand = last_short + (jnp.int32(1) << (12 - it))
                return jnp.where(count(equal & (pos <= cand)) < need, cand, last_short)

            limit = lax.fori_loop(0, 13, pos_bit, jnp.full((8, 1), -1, jnp.int32)) + 1
            bias_ref[rows8, :] = jnp.where(above | (equal & (pos <= limit)), 0.0, NEG_BIG)
        return carry

    @pl.when(jnp.max(cnt_ref[...]) > topk)
    def _():
        lax.fori_loop(0, tq // 8, break_ties, 0)

    rows = ATT_GROUP * ATT_ROWS
    tiles_per_span = ATT_SPAN // tk
    span_tiles = (n_key_tiles // tiles_per_span) * tiles_per_span
    n_spans = (jnp.minimum(n_tiles, span_tiles) + tiles_per_span - 1) // tiles_per_span
    chains = [(gl, qh) for gl in range(ATT_GROUPS_PER_STEP) for qh in range(tq // ATT_ROWS)]

    def group_step(gs, carry):
        kv = [gs * ATT_GROUPS_PER_STEP + gl for gl, _ in chains]
        q_rows = [slice(qh * ATT_ROWS, (qh + 1) * ATT_ROWS) for _, qh in chains]
        qs = [q_ref[0, pl.ds(kv[c] * ATT_GROUP, ATT_GROUP), q_rows[c], :].reshape(rows, HEAD_DIM)
              for c in range(len(chains))]
        m_ref[...] = jnp.full(m_ref.shape, NEG_BIG, _F32)
        acc_ref[...] = jnp.zeros(acc_ref.shape, _F32)

        def flash(start, width):
            s = [_dot_nt(qs[c], k_ref[0, kv[c], pl.ds(start, width), :]) for c in range(len(chains))]
            p = []
            for c in range(len(chains)):
                bias = bias_ref[q_rows[c], pl.ds(start, width)]
                sc = (s[c].reshape(ATT_GROUP, ATT_ROWS, width) + bias[None]).reshape(rows, width)
                m_prev = m_ref[c]
                m_new = jnp.maximum(m_prev, jnp.max(sc, axis=-1, keepdims=True))
                acc_ref[c] = jnp.exp2(m_prev - m_new) * acc_ref[c]
                m_ref[c] = m_new
                p.append(jnp.exp2(sc - m_new).astype(_MXU))
            for c in range(len(chains)):
                acc_ref[c] += jnp.dot(p[c], v_ref[0, kv[c], pl.ds(start, width), :],
                                      preferred_element_type=_F32)

        def span_step(j, carry):
            flash(pl.multiple_of(j * ATT_SPAN, ATT_SPAN), ATT_SPAN)
            return carry

        lax.fori_loop(0, n_spans, span_step, 0)
        if n_key_tiles > span_tiles:
            @pl.when(n_tiles > span_tiles)
            def _():
                flash(span_tiles * tk, (n_key_tiles - span_tiles) * tk)
        for c in range(len(chains)):
            acc = acc_ref[c]
            out = acc[:, :HEAD_DIM] / acc[:, HEAD_DIM:HEAD_DIM + 1]
            o_ref[0, pl.ds(kv[c] * ATT_GROUP, ATT_GROUP), q_rows[c], :] = (
                out.reshape(ATT_GROUP, ATT_ROWS, HEAD_DIM).astype(o_ref.dtype))
        return carry

    lax.fori_loop(0, ATT_KV_HEADS // ATT_GROUPS_PER_STEP, group_step, 0)


def _dsa_attn(q, k, v, qi, ki, wi, topk):
    b, _, tp, _ = q.shape
    tq = SEQ_ALIGN
    nq = tp // tq
    tiles_per_span = ATT_SPAN // tq
    classes = sorted(set(range(SELECT_CLASS_TILES, nq + 1, SELECT_CLASS_TILES)) | {nq})
    widths = tuple((n, n * tq, min(-(-n // tiles_per_span) * tiles_per_span, nq) * tq) for n in classes)
    n_chains = ATT_GROUPS_PER_STEP * (tq // ATT_ROWS)
    per_q = lambda n: pl.BlockSpec((1, n, tq, HEAD_DIM), lambda bi, i: (bi, 0, i, 0))
    per_b = lambda n, w: pl.BlockSpec((1, n, tp, w), lambda bi, i: (bi, 0, 0, 0),
                                      pipeline_mode=pl.Buffered(1))
    return pl.pallas_call(
        functools.partial(_dsa_attn_body, topk=topk, widths=widths),
        out_shape=jax.ShapeDtypeStruct((b, ATT_HEADS, tp, HEAD_DIM), _MXU),
        grid=(b, nq),
        in_specs=[per_q(IDX_HEADS),
                  pl.BlockSpec((1, tq, IDX_HEADS), lambda bi, i: (bi, i, 0)),
                  pl.BlockSpec((1, tp, IDX_DIM), lambda bi, i: (bi, 0, 0), pipeline_mode=pl.Buffered(1)),
                  per_q(ATT_HEADS), per_b(ATT_KV_HEADS, HEAD_DIM), per_b(ATT_KV_HEADS, LANES)],
        out_specs=per_q(ATT_HEADS),
        scratch_shapes=[pltpu.VMEM((tq, tp), jnp.int32),
                        pltpu.VMEM((tq, tp), _F32),
                        pltpu.VMEM((IDX_HEADS, tq, LANES), _F32),
                        pltpu.VMEM((n_chains, ATT_GROUP * ATT_ROWS, 1), _F32),
                        pltpu.VMEM((n_chains, ATT_GROUP * ATT_ROWS, LANES), _F32),
                        pltpu.VMEM((tq, 1), jnp.int32),
                        pltpu.VMEM((tq, 1), jnp.int32)],
        compiler_params=_params("parallel", "arbitrary"),
        name="dsa_attn",
    )(qi, wi, ki, q, k, v)


def _dsa_out_body(h_ref, y_ref, w_ref, o_ref):
    y = jnp.concatenate([y_ref[0, hd] for hd in range(ATT_HEADS)], axis=1)
    o_ref[0] = h_ref[0] + _dot(y, w_ref[...])


def _dsa_out(h, y, w_o):
    b, tp, d = h.shape
    tm = SEQ_ALIGN
    tile = pl.BlockSpec((1, tm, d), lambda bi, i: (bi, i, 0))
    return pl.pallas_call(
        _dsa_out_body,
        out_shape=jax.ShapeDtypeStruct((b, tp, d), _F32),
        grid=(b, tp // tm),
        in_specs=[tile, pl.BlockSpec((1, ATT_HEADS, tm, HEAD_DIM), lambda bi, i: (bi, 0, i, 0)),
                  _const_spec(w_o.shape)],
        out_specs=tile,
        compiler_params=_params("parallel", "parallel"),
        name="dsa_out",
    )(h, y, w_o.astype(_MXU))


def _dsa_mix(h, ng, w_in, q_g, k_g, topk):
    q, k, v, qi, ki, wi = _dsa_proj(h, ng, w_in, q_g, k_g)
    return _dsa_attn(q, k, v, qi, ki, wi, topk)


def _dsa_layer(h, ng, w_in, q_g, k_g, w_o, topk):
    return _dsa_out(h, _dsa_mix(h, ng, w_in, q_g, k_g, topk), w_o)


def kernel(x, meta, norm_g, ffn_w_in, ffn_w_out, rk_mix, rk_w_r, rk_w_k, rk_w_v, rk_w_o, rk_w0, rk_w1,
           rk_w2, rk_a0, rk_a1, rk_a2, rk_g1, rk_g2, rk_k_k, rk_k_a, rk_r_k, rk_lnx_g, rk_lnx_b,
           at_w_in, at_q_g, at_k_g, at_w_o):
    b, seq, d = x.shape
    topk = min(TOPK_MAX, seq // 4)
    t = seq + N_META
    tp = -(-t // SEQ_ALIGN) * SEQ_ALIGN
    h = jnp.concatenate([jnp.broadcast_to(meta.astype(x.dtype)[None], (b, N_META, d)), x,
                         jnp.zeros((b, tp - t, d), x.dtype)], axis=1)
    rk = dict(mix=rk_mix, w_r=rk_w_r, w_k=rk_w_k, w_v=rk_w_v, w_o=rk_w_o, w0=rk_w0, w1=rk_w1, w2=rk_w2,
              a0=rk_a0, a1=rk_a1, a2=rk_a2, g1=rk_g1, g2=rk_g2, k_k=rk_k_k, k_a=rk_k_a, r_k=rk_r_k,
              lnx_g=rk_lnx_g, lnx_b=rk_lnx_b)
    n_mixers = 2
    depth = norm_g.shape[0]
    flat = lambda a: a.reshape(b * tp, d)
    for layer in range(depth):
        g = norm_g[layer]
        last = layer + 1 == depth
        w_in2, w_out2 = ffn_w_in[layer, 1], ffn_w_out[layer, 1]
        h = _ffn(flat(h), g[0], ffn_w_in[layer, 0], ffn_w_out[layer, 0]).reshape(b, tp, d)
        j = layer // n_mixers
        if layer % n_mixers == 0:
            p = {name: val[j] for name, val in rk.items()}
            if last:
                out = _ffn_tail(_rwkv_layer(h, g[1], p), g[2], w_in2, w_out2, seq)
            else:
                y, gate = _rwkv_mix(h, g[1], p)
                h = _ffn(flat(h), g[2], w_in2, w_out2, gated=(flat(y), flat(gate), p["w_o"])).reshape(b, tp, d)
        else:
            if last:
                y = _dsa_mix(h, g[1], at_w_in[j], at_q_g[j], at_k_g[j], topk)
                out = _ffn_tail(h, g[2], w_in2, w_out2, seq, heads=(y, at_w_o[j]))
            else:
                h = _dsa_layer(h, g[1], at_w_in[j], at_q_g[j], at_k_g[j], at_w_o[j], topk)
                h = _ffn(flat(h), g[2], w_in2, w_out2).reshape(b, tp, d)
    return out
```
